```python
import math
import jax
import jax.numpy as jnp
from jax import lax
import numpy as np

D_MODEL = 1024
BATCH = 1
SEQ = 16384
DEPTH = 4

GRID_W = 64
CTX_LEN = 256
CHUNK = 64
EPS = 1e-6
F_TINY = 1e-30
N_EVEN = (DEPTH + 1) // 2
N_ODD = DEPTH // 2
H_A = 4
DK_A = 128
DV_A = 128
WK_A = H_A * DK_A
WV_A = H_A * DV_A
CONV_K = 5
H_B = 4
DK_B = 128
DV_B = 128
WK_B = H_B * DK_B
WV_B = H_B * DV_B
EV_SIZES = (WK_A, WK_A, WV_A, WV_A, 2 * H_A, 2 * H_A, WK_B, 2 * WK_B, WV_B, WV_B)
EV_WIDTH = 2 * WK_A + 2 * WV_A + 4 * H_A + 3 * WK_B + 2 * WV_B
H_C = 4
DK_C = 256
DV_C = 512
WK_C = H_C * DK_C
WV_C = H_C * DV_C
OD_SIZES = (WK_C, WK_C, WV_C, WV_C)
OD_WIDTH = 2 * WK_C + 2 * WV_C
ROPE_BASE = 10000.0
N_EXPERTS = 16
D_EXPERT = 2816
EC_CAPACITY = 2

kernel_name = 'hybrid_gdn_hgrn2_retention_ecmoe_dit'


def _rms(x):
    xf = x.astype(jnp.float32)
    return xf * lax.rsqrt(jnp.mean(xf * xf, axis=-1, keepdims=True) + EPS)


def rmsnorm(x, g):
    return (_rms(x) * g.astype(jnp.float32)).astype(x.dtype)


def modulate(x, g, shift, scale):
    return rmsnorm(x, g) * (1 + scale[:, None, :]) + shift[:, None, :]


def _split(p, sizes):
    return jnp.split(p, np.cumsum(sizes)[:-1].tolist(), axis=-1)


def _heads(t, n_heads):
    b, n, w = t.shape
    return t.reshape(b, n, n_heads, w // n_heads).transpose(0, 2, 1, 3)


def _merge(t):
    b, h, n, d = t.shape
    return t.transpose(0, 2, 1, 3).reshape(b, n, h * d)


def _l2norm(t):
    return t * lax.rsqrt(jnp.sum(t * t, axis=-1, keepdims=True) + EPS)


def _short_conv(u, w):
    ch = u.shape[-1]
    y = lax.conv_general_dilated(u, w.astype(u.dtype)[:, None, :], window_strides=(1,),
                                 padding=[(CONV_K // 2, CONV_K // 2)],
                                 dimension_numbers=('NWC', 'WIO', 'NWC'), feature_group_count=ch)
    return jax.nn.silu(y)


def _rope_axis(u, pos):
    nf = u.shape[-1] // 2
    inv = ROPE_BASE ** (-jnp.arange(nf, dtype=jnp.float32) / nf)
    ang = pos.astype(jnp.float32)[:, None] * inv[None, :]
    cos, sin = jnp.cos(ang), jnp.sin(ang)
    u1, u2 = u[..., :nf], u[..., nf:]
    return jnp.concatenate([u1 * cos - u2 * sin, u1 * sin + u2 * cos], axis=-1)


def _rope_2d(t, rows, cols):
    half = t.shape[-1] // 2
    return jnp.concatenate([_rope_axis(t[..., :half], rows), _rope_axis(t[..., half:], cols)], axis=-1)


def _chunks(t):
    b, h, n = t.shape[:3]
    return t.reshape(b, h, n // CHUNK, CHUNK, *t.shape[3:])


def _chunk_decay(gc):
    incl = jnp.tril(jnp.ones((CHUNK, CHUNK), bool))
    diff = gc[..., :, None] - gc[..., None, :]
    return jnp.where(incl, jnp.exp(jnp.where(incl, diff, 0.0)), 0.0)


def _run_chunks(step, s0, xs):
    s, o = lax.scan(step, s0, tuple(jnp.moveaxis(t, 2, 0) for t in xs))
    o = jnp.moveaxis(o, 0, 2)
    b, h, nc, c, dv = o.shape
    return o.reshape(b, h, nc * c, dv), s


def gdn_scan(q, k, v, g, beta, s0, reverse):
    if reverse:
        q, k, v, g, beta = (jnp.flip(t, 2) for t in (q, k, v, g, beta))
    q, k, v, g, beta = (_chunks(t) for t in (q, k, v, g, beta))
    gc = jnp.cumsum(g, axis=-1)
    decay = _chunk_decay(gc)
    strict = jnp.tril(jnp.ones((CHUNK, CHUNK), bool), -1)
    kb = k * beta[..., None]
    a = jnp.where(strict, jnp.einsum('bhntd,bhnsd->bhnts', kb, k) * decay, 0.0)
    rhs = jnp.concatenate([v * beta[..., None], kb * jnp.exp(gc)[..., None]], axis=-1)
    sol = lax.linalg.triangular_solve(a + jnp.eye(CHUNK, dtype=a.dtype), rhs, left_side=True,
                                      lower=True, unit_diagonal=True)
    dv = v.shape[-1]
    u, w = sol[..., :dv], sol[..., dv:]
    attn = jnp.einsum('bhntd,bhnsd->bhnts', q, k) * decay
    qe = q * jnp.exp(gc)[..., None]
    kt = k * jnp.exp(gc[..., -1:] - gc)[..., None]
    cd = jnp.exp(gc[..., -1])

    def step(s, xs):
        qe_c, at_c, u_c, w_c, kt_c, cd_c = xs
        v_new = u_c - jnp.einsum('bhtd,bhde->bhte', w_c, s)
        o = jnp.einsum('bhtd,bhde->bhte', qe_c, s) + jnp.einsum('bhts,bhse->bhte', at_c, v_new)
        s = cd_c[..., None, None] * s + jnp.einsum('bhsd,bhse->bhde', kt_c, v_new)
        return s, o

    o, s = _run_chunks(step, s0, (qe, attn, u, w, kt, cd))
    return (jnp.flip(o, 2) if reverse else o), s


def decay_scan(q, k, v, g, s0, reverse):
    if reverse:
        q, k, v, g = (jnp.flip(t, 2) for t in (q, k, v, g))
    q, k, v, g = (_chunks(t) for t in (q, k, v, g))
    gc = jnp.cumsum(g, axis=-1)
    attn = jnp.einsum('bhntd,bhnsd->bhnts', q, k) * _chunk_decay(gc)
    qe = q * jnp.exp(gc)[..., None]
    kt = k * jnp.exp(gc[..., -1:] - gc)[..., None]
    cd = jnp.exp(gc[..., -1])

    def step(s, xs):
        qe_c, at_c, v_c, kt_c, cd_c = xs
        o = jnp.einsum('bhtd,bhde->bhte', qe_c, s) + jnp.einsum('bhts,bhse->bhte', at_c, v_c)
        s = cd_c[..., None, None] * s + jnp.einsum('bhsd,bhse->bhde', kt_c, v_c)
        return s, o

    o, s = _run_chunks(step, s0, (qe, attn, v, kt, cd))
    return (jnp.flip(o, 2) if reverse else o), s


def gla_scan(q, k, v, g, s0, reverse):
    if reverse:
        q, k, v, g = (jnp.flip(t, 2) for t in (q, k, v, g))
    q, k, v, g = (_chunks(t) for t in (q, k, v, g))
    gc = jnp.cumsum(g, axis=3)
    qe = q * jnp.exp(gc)
    kt = k * jnp.exp(gc[:, :, :, -1:, :] - gc)
    cd = jnp.exp(gc[:, :, :, -1, :])
    incl = jnp.tril(jnp.ones((CHUNK, CHUNK), bool))[:, :, None]

    def step(s, xs):
        q_c, k_c, gc_c, v_c, qe_c, kt_c, cd_c = xs
        diff = gc_c[:, :, :, None, :] - gc_c[:, :, None, :, :]
        dec = jnp.where(incl, jnp.exp(jnp.where(incl, diff, 0.0)), 0.0)
        attn = jnp.einsum('bhtd,bhsd,bhtsd->bhts', q_c, k_c, dec)
        o = jnp.einsum('bhtd,bhde->bhte', qe_c, s) + jnp.einsum('bhts,bhse->bhte', attn, v_c)
        s = cd_c[..., :, None] * s + jnp.einsum('bhsd,bhse->bhde', kt_c, v_c)
        return s, o

    o, s = _run_chunks(step, s0, (q, k, gc, v, qe, kt, cd))
    return (jnp.flip(o, 2) if reverse else o), s


def _ctx_then_latent(scan_fn, ctx_args, lat_args, reverse):
    b, h = ctx_args[0].shape[:2]
    s0 = jnp.zeros((b, h, ctx_args[1].shape[-1], ctx_args[2].shape[-1]), jnp.float32)
    o_c, s_c = scan_fn(*ctx_args, s0, reverse)
    o_l, _ = scan_fn(*lat_args, s_c, reverse)
    return o_c, o_l


def _bidirectional(scan_fn, ctx_args_fn, lat_args_fn):
    outs = [_ctx_then_latent(scan_fn, ctx_args_fn(d), lat_args_fn(d), d == 1) for d in range(2)]
    return outs[0][0] + outs[1][0], outs[0][1] + outs[1][1]


def _gated_head_norm(o, z, w):
    return _merge(_rms(o) * w.astype(jnp.float32)) * jax.nn.silu(z)


def _hgrn_lower_bounds(raw):
    p = jax.nn.softmax(raw.astype(jnp.float32), axis=1)
    return jnp.cumsum(p, axis=1) - p[:, :1]


def _even_features(h, w_in, conv_w, a_log, dt_bias, lb):
    b, n, _ = h.shape
    p = jnp.einsum('bnd,de->bne', h, w_in).astype(jnp.float32)
    qa, ka, va, za, aa, ba, qb, fb, ib, gb = _split(p, EV_SIZES)
    qa, ka, va = _split(_short_conv(jnp.concatenate([qa, ka, va], axis=-1), conv_w), (WK_A, WK_A, WV_A))
    per_dir = lambda t: t.reshape(b, n, 2, H_A).transpose(2, 0, 3, 1)
    a_log = a_log.astype(jnp.float32)
    dt_bias = dt_bias.astype(jnp.float32)
    g_a = -jnp.exp(a_log)[:, None, :, None] * jax.nn.softplus(per_dir(aa) + dt_bias[:, None, :, None])
    beta = jax.nn.sigmoid(per_dir(ba))
    fb = fb.reshape(b, n, 2, H_B, DK_B).transpose(2, 0, 3, 1, 4)
    lb = lb.astype(jnp.float32).reshape(2, 1, H_B, 1, DK_B)
    f = lb + (1.0 - lb) * jax.nn.sigmoid(fb)
    log_f = jnp.log(jnp.maximum(f, F_TINY))
    k_b = (1.0 - lb) * jax.nn.sigmoid(-fb)
    return dict(qa=_l2norm(_heads(qa, H_A)) * DK_A ** -0.5, ka=_l2norm(_heads(ka, H_A)),
                va=_heads(va, H_A), za=za, ga=g_a, beta=beta,
                qb=_heads(jax.nn.silu(qb), H_B) * DK_B ** -0.5, kb=k_b, vb=_heads(ib, H_B),
                gfb=log_f, gb=gb)


def _even_mixer(hc, hl, w_in, conv_w, a_log, dt_bias, gdn_norm, lb, hgrn_norm, w_out, need_ctx):
    fc = _even_features(hc, w_in, conv_w, a_log, dt_bias, lb)
    fl = _even_features(hl, w_in, conv_w, a_log, dt_bias, lb)
    gdn_args = lambda f: (lambda d: (f['qa'], f['ka'], f['va'], f['ga'][d], f['beta'][d]))
    hgrn_args = lambda f: (lambda d: (f['qb'], f['kb'][d], f['vb'], f['gfb'][d]))
    oa_c, oa_l = _bidirectional(gdn_scan, gdn_args(fc), gdn_args(fl))
    ob_c, ob_l = _bidirectional(gla_scan, hgrn_args(fc), hgrn_args(fl))

    def project(f, oa, ob, dtype):
        y = jnp.concatenate([_gated_head_norm(oa, f['za'], gdn_norm),
                             _gated_head_norm(ob, f['gb'], hgrn_norm)], axis=-1)
        return jnp.einsum('bne,ed->bnd', y.astype(dtype), w_out)

    y_lat = project(fl, oa_l, ob_l, hl.dtype)
    y_ctx = project(fc, oa_c, ob_c, hc.dtype) if need_ctx else None
    return y_ctx, y_lat


def _ret_log_decay(reverse):
    e = np.arange(H_C)[::-1].copy() if reverse else np.arange(H_C)
    return jnp.log(1.0 - 2.0 ** (-5.0 - jnp.asarray(e, jnp.float32)))


def _ret_features(h, w_in, pos):
    p = jnp.einsum('bnd,de->bne', h, w_in).astype(jnp.float32)
    q, k, v, g = _split(p, OD_SIZES)
    q, k, v = _heads(q, H_C), _heads(k, H_C) * DK_C ** -0.5, _heads(v, H_C)
    if pos is not None:
        q, k = _rope_2d(q, *pos), _rope_2d(k, *pos)
    return q, k, v, g


def _retention_mixer(hc, hl, w_in, w_out, pos, need_ctx):
    qc, kc, vc, gc = _ret_features(hc, w_in, None)
    ql, kl, vl, gl = _ret_features(hl, w_in, pos)
    args = lambda q, k, v: (lambda d: (q, k, v, jnp.broadcast_to(_ret_log_decay(d == 1)[None, :, None], q.shape[:3])))
    o_c, o_l = _bidirectional(decay_scan, args(qc, kc, vc), args(ql, kl, vl))

    def project(o, g, dtype):
        y = _merge(_rms(o)) * jax.nn.silu(g)
        return jnp.einsum('bne,ed->bnd', y.astype(dtype), w_out)

    y_lat = project(o_l, gl, hl.dtype)
    y_ctx = project(o_c, gc, hc.dtype) if need_ctx else None
    return y_ctx, y_lat


def _ec_moe(h, router, w_gate, w_up, w_down):
    b, n, _ = h.shape
    cap = EC_CAPACITY * n // N_EXPERTS
    aff = jax.nn.softmax(jnp.einsum('bnd,de->bne', h, router).astype(jnp.float32), axis=-1)
    gate, idx = lax.top_k(jnp.swapaxes(aff, 1, 2), cap)
    bidx = jnp.arange(b)[:, None, None]
    xs = h[bidx, idx]
    hid = jax.nn.silu(jnp.einsum('becd,edf->becf', xs, w_gate)) * jnp.einsum('becd,edf->becf', xs, w_up)
    ys = jnp.einsum('becf,efd->becd', hid, w_down) * gate[..., None].astype(h.dtype)
    return jnp.zeros_like(h).at[bidx, idx].add(ys)


def setup_inputs(seed: int = 0) -> dict:
    key = jax.random.key(seed)
    ks = jax.random.split(key, 24)
    f32 = jnp.float32
    nrm = lambda k, shape, s: jax.random.normal(k, shape, f32) * s
    dt = jnp.exp(jax.random.uniform(ks[11], (N_EVEN, 2, H_A), f32, math.log(1e-3), math.log(1e-1)))
    return {
        'x': nrm(ks[0], (BATCH, SEQ, D_MODEL), 1.0),
        'c': nrm(ks[1], (BATCH, D_MODEL), 1.0),
        'ctx': nrm(ks[2], (BATCH, CTX_LEN, D_MODEL), 1.0),
        'c_ctx': nrm(ks[3], (D_MODEL,), 1.0),
        'ada_w': nrm(ks[4], (DEPTH, D_MODEL, 6 * D_MODEL), 0.5 * D_MODEL ** -0.5),
        'ada_b': nrm(ks[5], (DEPTH, 6 * D_MODEL), 0.02),
        'norm_g': 1.0 + nrm(ks[6], (DEPTH, 2, D_MODEL), 0.02),
        'final_g': 1.0 + nrm(ks[7], (D_MODEL,), 0.02),
        'ev_w_in': nrm(ks[8], (N_EVEN, D_MODEL, EV_WIDTH), D_MODEL ** -0.5),
        'ev_conv': nrm(ks[9], (N_EVEN, CONV_K, 2 * WK_A + WV_A), CONV_K ** -0.5),
        'ev_a_log': jnp.log(jax.random.uniform(ks[10], (N_EVEN, 2, H_A), f32, 1.0, 16.0)),
        'ev_dt_bias': dt + jnp.log(-jnp.expm1(-dt)),
        'ev_gdn_norm': 1.0 + nrm(ks[12], (N_EVEN, DV_A), 0.02),
        'ev_hgrn_lb': nrm(ks[13], (2, N_EVEN, WK_B), 0.1),
        'ev_hgrn_norm': 1.0 + nrm(ks[14], (N_EVEN, DV_B), 0.02),
        'ev_w_out': nrm(ks[15], (N_EVEN, WV_A + WV_B, D_MODEL), (WV_A + WV_B) ** -0.5),
        'od_w_in': nrm(ks[16], (N_ODD, D_MODEL, OD_WIDTH), D_MODEL ** -0.5),
        'od_w_out': nrm(ks[17], (N_ODD, WV_C, D_MODEL), WV_C ** -0.5),
        'moe_router': nrm(ks[18], (DEPTH, D_MODEL, N_EXPERTS), D_MODEL ** -0.5),
        'moe_w_gate': nrm(ks[19], (DEPTH, N_EXPERTS, D_MODEL, D_EXPERT), D_MODEL ** -0.5),
        'moe_w_up': nrm(ks[20], (DEPTH, N_EXPERTS, D_MODEL, D_EXPERT), D_MODEL ** -0.5),
        'moe_w_down': nrm(ks[21], (DEPTH, N_EXPERTS, D_EXPERT, D_MODEL), D_EXPERT ** -0.5),
    }


def reference(x, c, ctx, c_ctx, ada_w, ada_b, norm_g, final_g, ev_w_in, ev_conv, ev_a_log, ev_dt_bias,
              ev_gdn_norm, ev_hgrn_lb, ev_hgrn_norm, ev_w_out, od_w_in, od_w_out, moe_router,
              moe_w_gate, moe_w_up, moe_w_down):
    n_lat = x.shape[1]
    ROWS = n_lat // GRID_W
    rows = jnp.repeat(jnp.arange(ROWS), GRID_W)
    cols = jnp.tile(jnp.arange(GRID_W), ROWS)
    lbs = _hgrn_lower_bounds(ev_hgrn_lb)
    sc = jax.nn.silu(c)
    scc = jax.nn.silu(c_ctx)[None, :]
    xl, xc = x, ctx
    for l in range(DEPTH):
        last = l == DEPTH - 1
        i = l // 2
        mod_l = jnp.split(sc @ ada_w[l] + ada_b[l], 6, axis=-1)
        mod_c = jnp.split(scc @ ada_w[l] + ada_b[l], 6, axis=-1)
        hl = modulate(xl, norm_g[l, 0], mod_l[0], mod_l[1])
        hc = modulate(xc, norm_g[l, 0], mod_c[0], mod_c[1])
        if l % 2 == 0:
            yc, yl = _even_mixer(hc, hl, ev_w_in[i], ev_conv[i], ev_a_log[i], ev_dt_bias[i], ev_gdn_norm[i],
                                 lbs[:, i], ev_hgrn_norm[i], ev_w_out[i], not last)
        else:
            yc, yl = _retention_mixer(hc, hl, od_w_in[i], od_w_out[i], (rows, cols), not last)
        xl = xl + mod_l[2][:, None, :] * yl
        xl = xl + mod_l[5][:, None, :] * _ec_moe(modulate(xl, norm_g[l, 1], mod_l[3], mod_l[4]),
                                                 moe_router[l], moe_w_gate[l], moe_w_up[l], moe_w_down[l])
        if not last:
            xc = xc + mod_c[2][:, None, :] * yc
            xc = xc + mod_c[5][:, None, :] * _ec_moe(modulate(xc, norm_g[l, 1], mod_c[3], mod_c[4]),
                                                     moe_router[l], moe_w_gate[l], moe_w_up[l], moe_w_down[l])
    return rmsnorm(xl, final_g)
```

```python
import functools
import math

import numpy as np
import jax
import jax.numpy as jnp
from jax import lax
from jax.experimental import pallas as pl
from jax.experimental.pallas import tpu as pltpu

F32 = jnp.float32
BF16 = jnp.bfloat16
HI = lax.Precision.HIGHEST

GRID_W = 64
EPS = 1e-6
F_TINY = 1e-30
H_A, DK_A, DV_A = 4, 128, 128
CONV_K = 5
H_B, DK_B, DV_B = 4, 128, 128
H_C, DK_C, DV_C = 4, 256, 512
ROPE_BASE = 10000.0
N_EXPERTS = 16
EC_CAPACITY = 2

LANES = 128
SUBLANES = 8
ROW_TILE = 256
SCAN_T = 64
SUB_T = 16
RET_T = 256
FFN_TILE = 256
FFN_ROWS = 512
VMEM_LIMIT = 56 * 1024 * 1024


def _cparams(sem):
    return pltpu.CompilerParams(dimension_semantics=sem, vmem_limit_bytes=VMEM_LIMIT)


def _mm(a, b):
    return jnp.dot(a.astype(BF16), b.astype(BF16), preferred_element_type=F32)


def _mm_nt(a, b):
    return lax.dot_general(a.astype(BF16), b.astype(BF16), (((1,), (1,)), ((), ())),
                           preferred_element_type=F32)


def _mm_tn(a, b):
    return lax.dot_general(a.astype(BF16), b.astype(BF16), (((0,), (0,)), ((), ())),
                           preferred_element_type=F32)


def _mm32(a, b):
    return jnp.dot(a, b, precision=HI, preferred_element_type=F32)


def _sigmoid(x):
    return 1.0 / (1.0 + jnp.exp(-x))


def _silu(x):
    return x * _sigmoid(x)


def _softplus(x):
    return jnp.maximum(x, 0.0) + jnp.log(1.0 + jnp.exp(-jnp.abs(x)))


def _modulated(x, g, shift, scale):
    ms = jnp.mean(x * x, axis=-1, keepdims=True)
    return x * lax.rsqrt(ms + EPS) * g * (1.0 + scale) + shift


def _tri(n, reverse, strict):
    t = lax.broadcasted_iota(jnp.int32, (n, n), 0)
    s = lax.broadcasted_iota(jnp.int32, (n, n), 1)
    if reverse:
        return (s > t) if strict else (s >= t)
    return (s < t) if strict else (s <= t)


def _stream_of(n_ctx_tiles):
    return lambda i: jnp.where(i >= n_ctx_tiles, 1, 0)


def _bwd_chunk(n_ctx, n_all):
    return lambda i: jnp.where(i < n_ctx, n_ctx - 1 - i, n_all + n_ctx - 1 - i)


def _dense_body(a_ref, w_ref, b_ref, o_ref):
    o_ref[...] = _mm(a_ref[...], w_ref[...]) + b_ref[...]


def _dense(a, w, b, tn):
    m, k = a.shape
    n = w.shape[1]
    return pl.pallas_call(
        _dense_body,
        out_shape=jax.ShapeDtypeStruct((m, n), F32),
        grid=(n // tn,),
        in_specs=[pl.BlockSpec((m, k), lambda j: (0, 0)),
                  pl.BlockSpec((k, tn), lambda j: (0, j)),
                  pl.BlockSpec((1, tn), lambda j: (0, j))],
        out_specs=pl.BlockSpec((m, tn), lambda j: (0, j)),
        compiler_params=_cparams(("arbitrary",)),
        name="ada_dense",
    )(a, w, b)


def _proj_body(x_ref, g_ref, mod_ref, w_ref, o_ref):
    m = mod_ref[0]
    h = _modulated(x_ref[...], g_ref[...], m[0:1], m[1:2])
    o_ref[...] = jnp.dot(h.astype(BF16), w_ref[...], preferred_element_type=F32)


def _proj(x, g, mod, w, n_ctx_tiles):
    nt, d = x.shape
    n = w.shape[1]
    stream = _stream_of(n_ctx_tiles)
    return pl.pallas_call(
        _proj_body,
        out_shape=jax.ShapeDtypeStruct((nt, n), F32),
        grid=(nt // ROW_TILE,),
        in_specs=[pl.BlockSpec((ROW_TILE, d), lambda i: (i, 0)),
                  pl.BlockSpec((1, d), lambda i: (0, 0)),
                  pl.BlockSpec((1, 8, d), lambda i: (stream(i), 0, 0)),
                  pl.BlockSpec((d, n), lambda i: (0, 0))],
        out_specs=pl.BlockSpec((ROW_TILE, n), lambda i: (i, 0)),
        compiler_params=_cparams(("parallel",)),
        name="in_proj",
    )(x, g, mod, w)


def _feat_body(cur_ref, prev_ref, next_ref, w_ref, o_ref, ext_ref, *, n_ctx_tiles, n_tiles):
    i = pl.program_id(0)
    halo = SUBLANES
    prev_ok = jnp.logical_and(i != 0, i != n_ctx_tiles)
    next_ok = jnp.logical_and(i != n_ctx_tiles - 1, i != n_tiles - 1)
    ext_ref[0:halo, :] = jnp.where(prev_ok, prev_ref[...], 0.0)
    ext_ref[halo:halo + ROW_TILE, :] = cur_ref[...]
    ext_ref[halo + ROW_TILE:, :] = jnp.where(next_ok, next_ref[...], 0.0)
    acc = None
    for j in range(CONV_K):
        start = halo - CONV_K // 2 + j
        term = ext_ref[start:start + ROW_TILE, :] * w_ref[j:j + 1, :]
        acc = term if acc is None else acc + term
    y = _silu(acc)
    wq = H_A * DK_A
    for h in range(2 * H_A):
        lo = h * DK_A
        blk = y[:, lo:lo + DK_A]
        nrm = blk * lax.rsqrt(jnp.sum(blk * blk, axis=-1, keepdims=True) + EPS)
        if lo < wq:
            nrm = nrm * (DK_A ** -0.5)
        o_ref[:, lo:lo + DK_A] = nrm
    o_ref[:, 2 * wq:] = y[:, 2 * wq:]


def _even_feat(p, conv_w, n_ctx_tiles):
    nt = p.shape[0]
    wc = conv_w.shape[1]
    n_tiles = nt // ROW_TILE
    per = ROW_TILE // SUBLANES
    last_blk = nt // SUBLANES - 1
    return pl.pallas_call(
        functools.partial(_feat_body, n_ctx_tiles=n_ctx_tiles, n_tiles=n_tiles),
        out_shape=jax.ShapeDtypeStruct((nt, wc), F32),
        grid=(n_tiles,),
        in_specs=[pl.BlockSpec((ROW_TILE, wc), lambda i: (i, 0)),
                  pl.BlockSpec((SUBLANES, wc), lambda i: (jnp.maximum(i * per - 1, 0), 0)),
                  pl.BlockSpec((SUBLANES, wc), lambda i: (jnp.minimum((i + 1) * per, last_blk), 0)),
                  pl.BlockSpec((8, wc), lambda i: (0, 0))],
        out_specs=pl.BlockSpec((ROW_TILE, wc), lambda i: (i, 0)),
        scratch_shapes=[pltpu.VMEM((ROW_TILE + 2 * SUBLANES, wc), F32)],
        compiler_params=_cparams(("parallel",)),
        name="even_feat",
    )(p, p, p, conv_w)


def _unit_tri_inverse(a, n):
    r = lax.broadcasted_iota(jnp.int32, (n, n), 0)
    c = lax.broadcasted_iota(jnp.int32, (n, n), 1)
    p = jnp.where(r == c, 1.0, 0.0) - a
    ak = a
    for _ in range(int(math.log2(n)) - 1):
        ak = _mm32(ak, ak)
        p = p + _mm32(p, ak)
    return p


def _gdn_dir(q_ref, k_ref, v_ref, sm_ref, par_ref, o_ref, s_ref, d):
    reverse = d == 1
    t_len = SCAN_T
    incl = _tri(t_len, reverse, False)
    strict = _tri(t_len, reverse, True)
    sm = sm_ref[...]
    g_all = par_ref[0:1, :] * _softplus(sm + par_ref[1:2, :])
    beta_all = _sigmoid(sm)
    gc = _mm32(jnp.where(incl, 1.0, 0.0), g_all)
    gc_t = gc.T
    last_row = 0 if reverse else t_len - 1
    for h in range(H_A):
        c = d * H_A + h
        gcol = gc[:, c:c + 1]
        grow = gc_t[c:c + 1, :]
        dec = jnp.where(incl, jnp.exp(jnp.where(incl, gcol - grow, 0.0)), 0.0)
        beta = beta_all[:, 2 * H_A + c:2 * H_A + c + 1]
        lo = h * DK_A
        q = q_ref[:, lo:lo + DK_A]
        k = k_ref[:, lo:lo + DK_A]
        v = v_ref[:, lo:lo + DK_A]
        kb = k * beta
        a = jnp.where(strict, _mm_nt(kb, k) * dec, 0.0)
        a_inv = _unit_tri_inverse(a, t_len)
        eg = jnp.exp(gcol)
        u = _mm32(a_inv, v * beta)
        w = _mm32(a_inv, kb * eg)
        attn = _mm_nt(q, k) * dec
        glast = gc[last_row:last_row + 1, c:c + 1]
        kt = k * jnp.exp(glast - gcol)
        s = s_ref[c]
        v_new = u - _mm(w, s)
        o_ref[:, lo:lo + DK_A] = _mm(q * eg, s) + _mm(attn, v_new)
        s_ref[c] = jnp.exp(glast) * s + _mm_tn(kt, v_new)


def _gdn_body(qf, kf, vf, smf, qb, kb, vb, smb, par_ref, of_ref, ob_ref, s_ref):
    @pl.when(pl.program_id(0) == 0)
    def _():
        s_ref[...] = jnp.zeros_like(s_ref)

    _gdn_dir(qf, kf, vf, smf, par_ref, of_ref, s_ref, 0)
    _gdn_dir(qb, kb, vb, smb, par_ref, ob_ref, s_ref, 1)


def _gdn_scan(feat, p, par, small_blk, n_ctx):
    nt = feat.shape[0]
    n_all = nt // SCAN_T
    wk = H_A * DK_A
    bwd = _bwd_chunk(n_ctx, n_all)

    def specs(chunk):
        return [pl.BlockSpec((SCAN_T, wk), lambda i: (chunk(i), 0)),
                pl.BlockSpec((SCAN_T, wk), lambda i: (chunk(i), 1)),
                pl.BlockSpec((SCAN_T, wk), lambda i: (chunk(i), 2)),
                pl.BlockSpec((SCAN_T, LANES), lambda i: (chunk(i), small_blk))]

    fwd = lambda i: i
    out = jax.ShapeDtypeStruct((nt, H_A * DV_A), F32)
    return pl.pallas_call(
        _gdn_body,
        out_shape=(out, out),
        grid=(n_all,),
        in_specs=specs(fwd) + specs(bwd) + [pl.BlockSpec((8, LANES), lambda i: (0, 0))],
        out_specs=(pl.BlockSpec((SCAN_T, H_A * DV_A), lambda i: (i, 0)),
                   pl.BlockSpec((SCAN_T, H_A * DV_A), lambda i: (bwd(i), 0))),
        scratch_shapes=[pltpu.VMEM((2 * H_A, DK_A, DV_A), F32)],
        compiler_params=_cparams(("arbitrary",)),
        name="gdn_scan",
    )(feat, feat, feat, p, feat, feat, feat, p, par)


def _gla_diag(qi, ki, gi, reverse):
    n = SUB_T
    t3 = lax.broadcasted_iota(jnp.int32, (n, n, LANES), 0)
    s3 = lax.broadcasted_iota(jnp.int32, (n, n, LANES), 1)
    l3 = lax.broadcasted_iota(jnp.int32, (n, n, LANES), 2)
    ok = (s3 >= t3) if reverse else (s3 <= t3)
    diff = gi[:, None, :] - gi[None, :, :]
    x = jnp.where(ok, jnp.exp(jnp.where(ok, diff, 0.0)) * qi[:, None, :] * ki[None, :, :], 0.0)
    ones = jnp.ones((LANES, LANES), BF16)
    r = jnp.dot(x.reshape(n * n, LANES).astype(BF16), ones, preferred_element_type=F32)
    r3 = r.reshape(n, n, LANES)
    return jnp.sum(jnp.where(s3 == l3, r3, 0.0), axis=1)


def _gla_dir(q_ref, f_ref, v_ref, lb_ref, o_ref, s_ref, d):
    reverse = d == 1
    t_len = SCAN_T
    nb = t_len // SUB_T
    incl = _tri(t_len, reverse, False)
    cum = jnp.where(incl, 1.0, 0.0)
    last_row = 0 if reverse else t_len - 1
    for h in range(H_B):
        lo = h * DK_B
        fo = d * H_B * DK_B + lo
        fx = f_ref[:, fo:fo + DK_B]
        lb = lb_ref[d:d + 1, lo:lo + DK_B]
        f = lb + (1.0 - lb) * _sigmoid(fx)
        log_f = jnp.log(jnp.maximum(f, F_TINY))
        kk = (1.0 - lb) * _sigmoid(-fx)
        gc = _mm32(cum, log_f)
        q = _silu(q_ref[:, lo:lo + DK_B]) * (DK_B ** -0.5)
        v = v_ref[:, lo:lo + DV_B]
        glast = gc[last_row:last_row + 1, :]
        s_t = s_ref[d * H_B + h]
        inter = _mm_nt(q * jnp.exp(gc), s_t)
        for bi in range(nb):
            r0 = bi * SUB_T
            qi, ki, gi, vi = (x[r0:r0 + SUB_T] for x in (q, kk, gc, v))
            diag = _gla_diag(qi, ki, gi, reverse)
            o_blk = inter[r0:r0 + SUB_T] + _mm(diag[:, :SUB_T], vi)
            if reverse and bi < nb - 1:
                p0, p1 = r0 + SUB_T, t_len
                ref = gc[p0:p0 + 1, :]
            elif (not reverse) and bi > 0:
                p0, p1 = 0, r0
                ref = gc[r0 - 1:r0, :]
            else:
                p0 = p1 = 0
            if p1 > p0:
                qt = qi * jnp.exp(gi - ref)
                kt = kk[p0:p1] * jnp.exp(ref - gc[p0:p1])
                o_blk = o_blk + _mm(_mm_nt(qt, kt), v[p0:p1])
            o_ref[r0:r0 + SUB_T, lo:lo + DV_B] = o_blk
        k_tail = kk * jnp.exp(glast - gc)
        s_ref[d * H_B + h] = jnp.exp(glast) * s_t + _mm_tn(v, k_tail)


def _gla_body(qf, ff, vf, qb, fb, vb, lb_ref, of_ref, ob_ref, s_ref):
    @pl.when(pl.program_id(0) == 0)
    def _():
        s_ref[...] = jnp.zeros_like(s_ref)

    _gla_dir(qf, ff, vf, lb_ref, of_ref, s_ref, 0)
    _gla_dir(qb, fb, vb, lb_ref, ob_ref, s_ref, 1)


def _gla_scan(p, lb, q_blk, f_blk, v_blk, n_ctx):
    nt = p.shape[0]
    n_all = nt // SCAN_T
    wk = H_B * DK_B
    bwd = _bwd_chunk(n_ctx, n_all)

    def specs(chunk):
        return [pl.BlockSpec((SCAN_T, wk), lambda i: (chunk(i), q_blk)),
                pl.BlockSpec((SCAN_T, 2 * wk), lambda i: (chunk(i), f_blk)),
                pl.BlockSpec((SCAN_T, wk), lambda i: (chunk(i), v_blk))]

    fwd = lambda i: i
    out = jax.ShapeDtypeStruct((nt, H_B * DV_B), F32)
    return pl.pallas_call(
        _gla_body,
        out_shape=(out, out),
        grid=(n_all,),
        in_specs=specs(fwd) + specs(bwd) + [pl.BlockSpec((8, wk), lambda i: (0, 0))],
        out_specs=(pl.BlockSpec((SCAN_T, H_B * DV_B), lambda i: (i, 0)),
                   pl.BlockSpec((SCAN_T, H_B * DV_B), lambda i: (bwd(i), 0))),
        scratch_shapes=[pltpu.VMEM((2 * H_B, DV_B, DK_B), F32)],
        compiler_params=_cparams(("arbitrary",)),
        name="gla_scan",
    )(p, p, p, p, p, p, lb)


def _rope(x, cos, sin):
    parts = []
    for j in range(x.shape[1] // LANES):
        blk = x[:, j * LANES:(j + 1) * LANES]
        parts.append(blk * cos[:, j * LANES:(j + 1) * LANES]
                     + pltpu.roll(blk, LANES // 2, 1) * sin[:, j * LANES:(j + 1) * LANES])
    return jnp.concatenate(parts, axis=1)


def _ret_log_decay(d, h):
    e = (H_C - 1 - h) if d == 1 else h
    return math.log(1.0 - 2.0 ** (-5.0 - e))


def _ret_dir(q_ref, k_ref, v_ref, cos_ref, sin_ref, dec_ref, qs_ref, ks_ref, o_ref, s_ref, d):
    cos = cos_ref[...]
    sin = sin_ref[...]
    for h in range(H_C):
        c = d * H_C + h
        chunk_decay = math.exp(_ret_log_decay(d, h) * RET_T)
        q = _rope(q_ref[:, h * DK_C:(h + 1) * DK_C], cos, sin)
        k = _rope(k_ref[:, h * DK_C:(h + 1) * DK_C] * (DK_C ** -0.5), cos, sin)
        v = v_ref[:, h * DV_C:(h + 1) * DV_C]
        attn = _mm_nt(q, k) * dec_ref[c]
        qs = jnp.concatenate([qs_ref[c]] * (DK_C // LANES), axis=1)
        ks = jnp.concatenate([ks_ref[c]] * (DK_C // LANES), axis=1)
        s = s_ref[c]
        o_ref[:, h * DV_C:(h + 1) * DV_C] = _mm(q * qs, s) + _mm(attn, v)
        s_ref[c] = chunk_decay * s + _mm_tn(k * ks, v)


def _ret_body(qf, kf, vf, cf, sf, qb, kb, vb, cb, sb, dec_ref, qs_ref, ks_ref, of_ref, ob_ref, s_ref):
    @pl.when(pl.program_id(0) == 0)
    def _():
        s_ref[...] = jnp.zeros_like(s_ref)

    _ret_dir(qf, kf, vf, cf, sf, dec_ref, qs_ref, ks_ref, of_ref, s_ref, 0)
    _ret_dir(qb, kb, vb, cb, sb, dec_ref, qs_ref, ks_ref, ob_ref, s_ref, 1)


def _ret_tables():
    t = np.arange(RET_T, dtype=np.float64)
    dec, qs, ks = [], [], []
    for d in range(2):
        for h in range(H_C):
            lg = _ret_log_decay(d, h)
            pos = (RET_T - 1 - t) if d == 1 else t
            diff = pos[:, None] - pos[None, :]
            dec.append(np.where(diff >= 0, np.exp(lg * np.maximum(diff, 0.0)), 0.0))
            qs.append(np.broadcast_to(np.exp(lg * (pos + 1.0))[:, None], (RET_T, LANES)))
            ks.append(np.broadcast_to(np.exp(lg * (RET_T - 1.0 - pos))[:, None], (RET_T, LANES)))
    f = lambda xs: jnp.asarray(np.stack(xs), F32)
    return f(dec), f(qs), f(ks)


def _ret_scan(p, cos, sin, n_ctx):
    nt = p.shape[0]
    n_all = nt // RET_T
    wk, wv = H_C * DK_C, H_C * DV_C
    bwd = _bwd_chunk(n_ctx, n_all)
    dec, qs, ks = _ret_tables()

    def specs(chunk):
        return [pl.BlockSpec((RET_T, wk), lambda i: (chunk(i), 0)),
                pl.BlockSpec((RET_T, wk), lambda i: (chunk(i), 1)),
                pl.BlockSpec((RET_T, wv), lambda i: (chunk(i), 1)),
                pl.BlockSpec((RET_T, DK_C), lambda i: (chunk(i), 0)),
                pl.BlockSpec((RET_T, DK_C), lambda i: (chunk(i), 0))]

    full = lambda a: pl.BlockSpec(a.shape, lambda i: (0,) * a.ndim)
    fwd = lambda i: i
    out = jax.ShapeDtypeStruct((nt, wv), F32)
    return pl.pallas_call(
        _ret_body,
        out_shape=(out, out),
        grid=(n_all,),
        in_specs=specs(fwd) + specs(bwd) + [full(dec), full(qs), full(ks)],
        out_specs=(pl.BlockSpec((RET_T, wv), lambda i: (i, 0)),
                   pl.BlockSpec((RET_T, wv), lambda i: (bwd(i), 0))),
        scratch_shapes=[pltpu.VMEM((2 * H_C, DK_C, DV_C), F32)],
        compiler_params=_cparams(("arbitrary",)),
        name="ret_scan",
    )(p, p, p, cos, sin, p, p, p, cos, sin, dec, qs, ks)


def _outproj_body(*refs, groups, head_w):
    n_in = 3 * groups
    o_refs = refs[:n_in]
    gain_ref, w_ref, x_ref, mod_ref, g2_ref, r_ref = refs[n_in:n_in + 6]
    xo_ref, h_ref, lg_ref = refs[n_in + 6:]
    ys = []
    for gi in range(groups):
        of_ref, ob_ref, z_ref = o_refs[3 * gi:3 * gi + 3]
        wgrp = of_ref.shape[1]
        for h in range(wgrp // head_w):
            lo = h * head_w
            o = of_ref[:, lo:lo + head_w] + ob_ref[:, lo:lo + head_w]
            rms = o * lax.rsqrt(jnp.mean(o * o, axis=-1, keepdims=True) + EPS)
            gain = gain_ref[gi:gi + 1, 0:head_w]
            ys.append((rms * gain * _silu(z_ref[:, lo:lo + head_w])).astype(BF16))
    y = jnp.concatenate(ys, axis=1)
    m = mod_ref[0]
    x_new = x_ref[...] + m[2:3] * jnp.dot(y, w_ref[...], preferred_element_type=F32)
    xo_ref[...] = x_new
    h2 = _modulated(x_new, g2_ref[...], m[3:4], m[4:5])
    h_ref[...] = h2.astype(BF16)
    lg_ref[...] = _mm32(h2, r_ref[...])


def _outproj(o_groups, gains, w, x, mod, g2, router, head_w, n_ctx_tiles):
    nt, d = x.shape
    stream = _stream_of(n_ctx_tiles)
    in_specs, args = [], []
    for of, ob, z, zblk in o_groups:
        wg = of.shape[1]
        in_specs += [pl.BlockSpec((ROW_TILE, wg), lambda i: (i, 0)),
                     pl.BlockSpec((ROW_TILE, wg), lambda i: (i, 0)),
                     pl.BlockSpec((ROW_TILE, wg), lambda i, zblk=zblk: (i, zblk))]
        args += [of, ob, z]
    kdim = w.shape[0]
    in_specs += [pl.BlockSpec(gains.shape, lambda i: (0, 0)),
                 pl.BlockSpec((kdim, d), lambda i: (0, 0)),
                 pl.BlockSpec((ROW_TILE, d), lambda i: (i, 0)),
                 pl.BlockSpec((1, 8, d), lambda i: (stream(i), 0, 0)),
                 pl.BlockSpec((1, d), lambda i: (0, 0)),
                 pl.BlockSpec((d, LANES), lambda i: (0, 0))]
    args += [gains, w, x, mod, g2, router]
    row = lambda n: pl.BlockSpec((ROW_TILE, n), lambda i: (i, 0))
    return pl.pallas_call(
        functools.partial(_outproj_body, groups=len(o_groups), head_w=head_w),
        out_shape=(jax.ShapeDtypeStruct((nt, d), F32), jax.ShapeDtypeStruct((nt, d), BF16),
                   jax.ShapeDtypeStruct((nt, LANES), F32)),
        grid=(nt // ROW_TILE,),
        in_specs=in_specs,
        out_specs=(row(d), row(d), row(LANES)),
        compiler_params=_cparams(("parallel",)),
        name="out_proj",
    )(*args)


def _ffn_body(xs_ref, gate_ref, wg_ref, wu_ref, wd_ref, o_ref):
    f = pl.program_id(1)
    wg = wg_ref[0].astype(BF16)
    wu = wu_ref[0].astype(BF16)
    wd = wd_ref[0].astype(BF16)
    cap = xs_ref.shape[1]
    for r0 in range(0, cap, FFN_ROWS):
        r1 = min(r0 + FFN_ROWS, cap)
        xs = xs_ref[0, r0:r1, :]
        a = jnp.dot(xs, wg, preferred_element_type=F32)
        b = jnp.dot(xs, wu, preferred_element_type=F32)
        hid = (_silu(a) * b).astype(BF16)
        y = jnp.dot(hid, wd, preferred_element_type=F32)

        @pl.when(f == 0)
        def _():
            o_ref[0, r0:r1, :] = y

        @pl.when(f != 0)
        def _():
            o_ref[0, r0:r1, :] += y

    @pl.when(f == pl.num_programs(1) - 1)
    def _():
        o_ref[0] = o_ref[0] * gate_ref[0]


def _expert_ffn(xs, gate, w_gate, w_up, w_down):
    e, cap, d = xs.shape
    dff = w_gate.shape[2]
    return pl.pallas_call(
        _ffn_body,
        out_shape=jax.ShapeDtypeStruct((e, cap, d), F32),
        grid=(e, dff // FFN_TILE),
        in_specs=[pl.BlockSpec((1, cap, d), lambda i, f: (i, 0, 0)),
                  pl.BlockSpec((1, cap, 1), lambda i, f: (i, 0, 0)),
                  pl.BlockSpec((1, d, FFN_TILE), lambda i, f: (i, 0, f)),
                  pl.BlockSpec((1, d, FFN_TILE), lambda i, f: (i, 0, f)),
                  pl.BlockSpec((1, FFN_TILE, d), lambda i, f: (i, f, 0))],
        out_specs=pl.BlockSpec((1, cap, d), lambda i, f: (i, 0, 0)),
        compiler_params=_cparams(("parallel", "arbitrary")),
        name="expert_ffn",
    )(xs, gate, w_gate, w_up, w_down)


def _final_body(x_ref, g_ref, o_ref):
    x = x_ref[...]
    o_ref[...] = x * lax.rsqrt(jnp.mean(x * x, axis=-1, keepdims=True) + EPS) * g_ref[...]


def _final_norm(x, g):
    nt, d = x.shape
    return pl.pallas_call(
        _final_body,
        out_shape=jax.ShapeDtypeStruct((nt, d), F32),
        grid=(nt // ROW_TILE,),
        in_specs=[pl.BlockSpec((ROW_TILE, d), lambda i: (i, 0)), pl.BlockSpec((1, d), lambda i: (0, 0))],
        out_specs=pl.BlockSpec((ROW_TILE, d), lambda i: (i, 0)),
        compiler_params=_cparams(("parallel",)),
        name="final_norm",
    )(x, g)


def _route(logits, cap):
    aff = jax.nn.softmax(logits, axis=-1)
    return lax.top_k(aff.T, cap)


def _moe(h2, logits, n_ctx_rows, w_gate, w_up, w_down, with_ctx):
    nt, d = h2.shape
    n_lat = nt - n_ctx_rows
    lg = logits[:, :N_EXPERTS]
    cap_l = EC_CAPACITY * n_lat // N_EXPERTS
    gate_l, idx_l = _route(lg[n_ctx_rows:], cap_l)
    idx_l = idx_l + n_ctx_rows
    if with_ctx:
        cap_c = EC_CAPACITY * n_ctx_rows // N_EXPERTS
        gate_c, idx_c = _route(lg[:n_ctx_rows], cap_c)
        gate = jnp.concatenate([gate_l, gate_c], axis=1)
        idx = jnp.concatenate([idx_l, idx_c], axis=1)
    else:
        gate, idx = gate_l, idx_l
    xs = h2[idx]
    ys = _expert_ffn(xs, gate[..., None], w_gate, w_up, w_down)
    return jnp.zeros((nt, d), F32).at[idx.reshape(-1)].add(ys.reshape(-1, d))


def _rope_tables(n_ctx_rows, n_lat):
    quarter = DK_C // 4
    inv = ROPE_BASE ** (-jnp.arange(quarter, dtype=F32) / quarter)
    t = jnp.arange(n_lat)
    ang_r = (t // GRID_W).astype(F32)[:, None] * inv[None, :]
    ang_c = (t % GRID_W).astype(F32)[:, None] * inv[None, :]
    cos = jnp.concatenate([jnp.cos(ang_r)] * 2 + [jnp.cos(ang_c)] * 2, axis=1)
    sin = jnp.concatenate([-jnp.sin(ang_r), jnp.sin(ang_r), -jnp.sin(ang_c), jnp.sin(ang_c)], axis=1)
    cos = jnp.concatenate([jnp.ones((n_ctx_rows, DK_C), F32), cos], axis=0)
    sin = jnp.concatenate([jnp.zeros((n_ctx_rows, DK_C), F32), sin], axis=0)
    return cos, sin


def _pad_cols(a, n):
    return jnp.pad(a, ((0, 0), (0, n - a.shape[1])))


def kernel(x, c, ctx, c_ctx, ada_w, ada_b, norm_g, final_g, ev_w_in, ev_conv, ev_a_log, ev_dt_bias, ev_gdn_norm, ev_hgrn_lb, ev_hgrn_norm, ev_w_out, od_w_in, od_w_out, moe_router, moe_w_gate, moe_w_up, moe_w_down):
    depth = ada_w.shape[0]
    n_lat, d = x.shape[1], x.shape[2]
    n_ctx_rows = ctx.shape[1]
    assert x.shape[0] == 1 and n_ctx_rows % RET_T == 0 and n_lat % RET_T == 0 and RET_T == ROW_TILE
    n_ctx_tiles = n_ctx_rows // ROW_TILE

    xs = jnp.concatenate([ctx[0], x[0]], axis=0)
    cond = jnp.zeros((8, d), F32).at[0].set(jax.nn.silu(c_ctx)).at[1].set(jax.nn.silu(c[0]))
    p_l = jax.nn.softmax(ev_hgrn_lb.astype(F32), axis=1)
    lbs = jnp.cumsum(p_l, axis=1) - p_l[:, :1]
    cos, sin = _rope_tables(n_ctx_rows, n_lat)
    router = jnp.pad(moe_router, ((0, 0), (0, 0), (0, LANES - N_EXPERTS)))

    wka, wva, wkb, wvb = H_A * DK_A, H_A * DV_A, H_B * DK_B, H_B * DV_B
    n_small = 4 * H_A
    big = 2 * wka + 2 * wva + 3 * wkb + 2 * wvb
    small_at = 2 * wka + 2 * wva

    for l in range(depth):
        last = l == depth - 1
        i = l // 2
        mod = _dense(cond, ada_w[l].astype(BF16), ada_b[l][None, :], 6 * d // 4)
        mod = jnp.pad(mod[:2].reshape(2, 6, d), ((0, 0), (0, 2), (0, 0)))
        g1 = norm_g[l, 0][None, :]
        g2 = norm_g[l, 1][None, :]
        if l % 2 == 0:
            w_in = ev_w_in[i]
            qb_at = small_at + n_small
            fb_at = qb_at + wkb
            ib_at = fb_at + 2 * wkb
            w_perm = jnp.concatenate([w_in[:, :small_at], w_in[:, fb_at:ib_at], w_in[:, qb_at:fb_at],
                                      w_in[:, ib_at:],
                                      _pad_cols(w_in[:, small_at:small_at + n_small], LANES)], axis=1)
            p = _proj(xs, g1, mod, w_perm.astype(BF16), n_ctx_tiles)
            feat = _even_feat(p, jnp.pad(ev_conv[i], ((0, 8 - CONV_K), (0, 0))), n_ctx_tiles)
            par = jnp.zeros((8, LANES), F32)
            par = par.at[0, :2 * H_A].set(-jnp.exp(ev_a_log[i].astype(F32)).reshape(-1))
            par = par.at[1, :2 * H_A].set(ev_dt_bias[i].astype(F32).reshape(-1))
            oa_f, oa_b = _gdn_scan(feat, p, par, big // LANES, n_ctx_rows // SCAN_T)
            lb = jnp.pad(lbs[:, i], ((0, 6), (0, 0)))
            ob_f, ob_b = _gla_scan(p, lb, 6, 2, 7, n_ctx_rows // SCAN_T)
            gains = jnp.zeros((8, LANES), F32).at[0].set(ev_gdn_norm[i]).at[1].set(ev_hgrn_norm[i])
            groups = [(oa_f, oa_b, p, 3), (ob_f, ob_b, p, 8)]
            xs, h2, logits = _outproj(groups, gains, ev_w_out[i].astype(BF16), xs, mod, g2,
                                      router[l], DV_A, n_ctx_tiles)
        else:
            p = _proj(xs, g1, mod, od_w_in[i].astype(BF16), n_ctx_tiles)
            o_f, o_b = _ret_scan(p, cos, sin, n_ctx_rows // RET_T)
            gains = jnp.ones((8, DV_C), F32)
            groups = [(o_f, o_b, p, 2)]
            xs, h2, logits = _outproj(groups, gains, od_w_out[i].astype(BF16), xs, mod, g2,
                                      router[l], DV_C, n_ctx_tiles)
        moe = _moe(h2, logits, n_ctx_rows, moe_w_gate[l], moe_w_up[l], moe_w_down[l], not last)
        stream_gate = jnp.concatenate([jnp.broadcast_to(mod[0, 5], (n_ctx_rows, d)),
                                       jnp.broadcast_to(mod[1, 5], (n_lat, d))], axis=0)
        xs = xs + stream_gate * moe
    out = _final_norm(xs, final_g[None, :])
    return out[n_ctx_rows:][None]
```

```python
import functools
import math

import numpy as np
import jax
import jax.numpy as jnp
from jax import lax
from jax.experimental import pallas as pl
from jax.experimental.pallas import tpu as pltpu

F32 = jnp.float32
BF16 = jnp.bfloat16

GRID_W = 64
EPS = 1e-6
F_TINY = 1e-30
H_A, DK_A, DV_A = 4, 128, 128
CONV_K = 5
H_B, DK_B, DV_B = 4, 128, 128
H_C, DK_C, DV_C = 4, 256, 512
ROPE_BASE = 10000.0
N_EXPERTS = 16
EC_CAPACITY = 2

LANES = 128
SUBLANES = 8
ROW_TILE = 256
SCAN_T = 64
SUB_T = 16
RET_T = 256
FFN_TILE = 256
FFN_ROWS = 512
VMEM_LIMIT = 56 * 1024 * 1024


def _cparams(sem):
    return pltpu.CompilerParams(dimension_semantics=sem, vmem_limit_bytes=VMEM_LIMIT)


def _mm(a, b):
    return jnp.dot(a.astype(BF16), b.astype(BF16), preferred_element_type=F32)


def _mm_nt(a, b):
    return lax.dot_general(a.astype(BF16), b.astype(BF16), (((1,), (1,)), ((), ())),
                           preferred_element_type=F32)


def _mm_tn(a, b):
    return lax.dot_general(a.astype(BF16), b.astype(BF16), (((0,), (0,)), ((), ())),
                           preferred_element_type=F32)


def _split2(x):
    hi = x.astype(BF16)
    return hi, (x - hi.astype(F32)).astype(BF16)


def _mm3(a, b):
    ah, al = _split2(a)
    bh, bl = _split2(b)
    dot = lambda x, y: jnp.dot(x, y, preferred_element_type=F32)
    return dot(ah, bh) + (dot(ah, bl) + dot(al, bh))


def _cumdot(m, g):
    w = g.shape[1]
    hi = g.astype(BF16)
    r1 = g - hi.astype(F32)
    mid = r1.astype(BF16)
    lo = (r1 - mid.astype(F32)).astype(BF16)
    out = jnp.dot(m, jnp.concatenate([hi, mid, lo], axis=1), preferred_element_type=F32)
    return out[:, :w] + (out[:, w:2 * w] + out[:, 2 * w:])


def _sigmoid(x):
    return 1.0 / (1.0 + jnp.exp(-x))


def _silu(x):
    return x * _sigmoid(x)


def _softplus(x):
    return jnp.maximum(x, 0.0) + jnp.log(1.0 + jnp.exp(-jnp.abs(x)))


def _modulated(x, g, shift, scale):
    ms = jnp.mean(x * x, axis=-1, keepdims=True)
    return x * lax.rsqrt(ms + EPS) * g * (1.0 + scale) + shift


def _tri(n, reverse, strict):
    t = lax.broadcasted_iota(jnp.int32, (n, n), 0)
    s = lax.broadcasted_iota(jnp.int32, (n, n), 1)
    if reverse:
        return (s > t) if strict else (s >= t)
    return (s < t) if strict else (s <= t)


def _stream_of(n_ctx_tiles):
    return lambda i: jnp.where(i >= n_ctx_tiles, 1, 0)


def _bwd_chunk(n_ctx, n_all):
    return lambda i: jnp.where(i < n_ctx, n_ctx - 1 - i, n_all + n_ctx - 1 - i)


def _dense_body(a_ref, w_ref, b_ref, o_ref):
    o_ref[...] = _mm(a_ref[...], w_ref[...]) + b_ref[...]


def _dense(a, w, b, tn):
    m, k = a.shape
    n = w.shape[1]
    return pl.pallas_call(
        _dense_body,
        out_shape=jax.ShapeDtypeStruct((m, n), F32),
        grid=(n // tn,),
        in_specs=[pl.BlockSpec((m, k), lambda j: (0, 0)),
                  pl.BlockSpec((k, tn), lambda j: (0, j)),
                  pl.BlockSpec((1, tn), lambda j: (0, j))],
        out_specs=pl.BlockSpec((m, tn), lambda j: (0, j)),
        compiler_params=_cparams(("arbitrary",)),
        name="ada_dense",
    )(a, w, b)


def _proj_body(x_ref, g_ref, mod_ref, w_ref, o_ref):
    m = mod_ref[0]
    h = _modulated(x_ref[...], g_ref[...], m[0:1], m[1:2])
    o_ref[...] = jnp.dot(h.astype(BF16), w_ref[...], preferred_element_type=F32)


def _proj(x, g, mod, w, n_ctx_tiles):
    nt, d = x.shape
    n = w.shape[1]
    stream = _stream_of(n_ctx_tiles)
    return pl.pallas_call(
        _proj_body,
        out_shape=jax.ShapeDtypeStruct((nt, n), F32),
        grid=(nt // ROW_TILE,),
        in_specs=[pl.BlockSpec((ROW_TILE, d), lambda i: (i, 0)),
                  pl.BlockSpec((1, d), lambda i: (0, 0)),
                  pl.BlockSpec((1, 8, d), lambda i: (stream(i), 0, 0)),
                  pl.BlockSpec((d, n), lambda i: (0, 0))],
        out_specs=pl.BlockSpec((ROW_TILE, n), lambda i: (i, 0)),
        compiler_params=_cparams(("parallel",)),
        name="in_proj",
    )(x, g, mod, w)


def _feat_body(cur_ref, prev_ref, next_ref, w_ref, o_ref, ext_ref, *, n_ctx_tiles, n_tiles):
    i = pl.program_id(0)
    halo = SUBLANES
    prev_ok = jnp.logical_and(i != 0, i != n_ctx_tiles)
    next_ok = jnp.logical_and(i != n_ctx_tiles - 1, i != n_tiles - 1)
    ext_ref[0:halo, :] = jnp.where(prev_ok, prev_ref[...], 0.0)
    ext_ref[halo:halo + ROW_TILE, :] = cur_ref[...]
    ext_ref[halo + ROW_TILE:, :] = jnp.where(next_ok, next_ref[...], 0.0)
    acc = None
    for j in range(CONV_K):
        start = halo - CONV_K // 2 + j
        term = ext_ref[start:start + ROW_TILE, :] * w_ref[j:j + 1, :]
        acc = term if acc is None else acc + term
    y = _silu(acc)
    wq = H_A * DK_A
    for h in range(2 * H_A):
        lo = h * DK_A
        blk = y[:, lo:lo + DK_A]
        nrm = blk * lax.rsqrt(jnp.sum(blk * blk, axis=-1, keepdims=True) + EPS)
        if lo < wq:
            nrm = nrm * (DK_A ** -0.5)
        o_ref[:, lo:lo + DK_A] = nrm
    o_ref[:, 2 * wq:] = y[:, 2 * wq:]


def _even_feat(p, conv_w, n_ctx_tiles):
    nt = p.shape[0]
    wc = conv_w.shape[1]
    n_tiles = nt // ROW_TILE
    per = ROW_TILE // SUBLANES
    last_blk = nt // SUBLANES - 1
    return pl.pallas_call(
        functools.partial(_feat_body, n_ctx_tiles=n_ctx_tiles, n_tiles=n_tiles),
        out_shape=jax.ShapeDtypeStruct((nt, wc), F32),
        grid=(n_tiles,),
        in_specs=[pl.BlockSpec((ROW_TILE, wc), lambda i: (i, 0)),
                  pl.BlockSpec((SUBLANES, wc), lambda i: (jnp.maximum(i * per - 1, 0), 0)),
                  pl.BlockSpec((SUBLANES, wc), lambda i: (jnp.minimum((i + 1) * per, last_blk), 0)),
                  pl.BlockSpec((8, wc), lambda i: (0, 0))],
        out_specs=pl.BlockSpec((ROW_TILE, wc), lambda i: (i, 0)),
        scratch_shapes=[pltpu.VMEM((ROW_TILE + 2 * SUBLANES, wc), F32)],
        compiler_params=_cparams(("parallel",)),
        name="even_feat",
    )(p, p, p, conv_w)


def _unit_tri_inverses(mats, n):
    r = lax.broadcasted_iota(jnp.int32, (n, n), 0)
    c = lax.broadcasted_iota(jnp.int32, (n, n), 1)
    eye = jnp.where(r == c, 1.0, 0.0)
    ps = [eye - a for a in mats]
    aks = [_mm3(a, a) for a in mats]
    for _ in range(int(math.log2(n)) - 2):
        outs = [_mm3(jnp.concatenate([ak, p], axis=0), ak) for ak, p in zip(aks, ps)]
        aks = [o[:n] for o in outs]
        ps = [p + o[n:] for p, o in zip(ps, outs)]
    return [p + _mm3(p, ak) for p, ak in zip(ps, aks)]


def _gdn_body(qf, kf, vf, smf, qb, kb, vb, smb, par_ref, of_ref, ob_ref, s_ref):
    @pl.when(pl.program_id(0) == 0)
    def _():
        s_ref[...] = jnp.zeros_like(s_ref)

    t_len = SCAN_T
    chains = [(d, h) for d in range(2) for h in range(H_A)]
    refs = ((qf, kf, vf, smf, of_ref), (qb, kb, vb, smb, ob_ref))
    incl = [_tri(t_len, d == 1, False) for d in range(2)]
    strict = [_tri(t_len, d == 1, True) for d in range(2)]
    gcs, gcts, betas = [], [], []
    for d in range(2):
        sm = refs[d][3][...]
        g_all = par_ref[0:1, :] * _softplus(sm + par_ref[1:2, :])
        betas.append(_sigmoid(sm))
        gc = _cumdot(jnp.where(incl[d], 1.0, 0.0).astype(BF16), g_all)
        gcs.append(gc)
        gcts.append(gc.T)
    st = []
    for d, h in chains:
        c = d * H_A + h
        lo = h * DK_A
        gcol = gcs[d][:, c:c + 1]
        grow = gcts[d][c:c + 1, :]
        dec = jnp.where(incl[d], jnp.exp(jnp.where(incl[d], gcol - grow, 0.0)), 0.0)
        beta = betas[d][:, 2 * H_A + c:2 * H_A + c + 1]
        q = refs[d][0][:, lo:lo + DK_A]
        k = refs[d][1][:, lo:lo + DK_A]
        v = refs[d][2][:, lo:lo + DK_A]
        last_row = 0 if d == 1 else t_len - 1
        glast = gcs[d][last_row:last_row + 1, c:c + 1]
        st.append(dict(c=c, lo=lo, d=d, dec=dec, beta=beta, q=q, k=k, v=v, kb=k * beta, eg=jnp.exp(gcol),
                       kt=k * jnp.exp(glast - gcol), cd=jnp.exp(glast)))
    kks = [_mm_nt(x["kb"], x["k"]) for x in st]
    qks = [_mm_nt(x["q"], x["k"]) for x in st]
    invs = _unit_tri_inverses([jnp.where(strict[x["d"]], kk * x["dec"], 0.0) for x, kk in zip(st, kks)], t_len)
    uws = [_mm3(inv, jnp.concatenate([x["v"] * x["beta"], x["kb"] * x["eg"]], axis=1)) for x, inv in zip(st, invs)]
    ss = [s_ref[x["c"]] for x in st]
    vns = [uw[:, :DV_A] - _mm(uw[:, DV_A:], s) for uw, s in zip(uws, ss)]
    inters = [_mm(x["q"] * x["eg"], s) for x, s in zip(st, ss)]
    intras = [_mm(qk * x["dec"], vn) for x, qk, vn in zip(st, qks, vns)]
    upds = [_mm_tn(x["kt"], vn) for x, vn in zip(st, vns)]
    for x, s, inter, intra, upd in zip(st, ss, inters, intras, upds):
        refs[x["d"]][4][:, x["lo"]:x["lo"] + DV_A] = inter + intra
        s_ref[x["c"]] = x["cd"] * s + upd


def _gdn_scan(feat, p, par, small_blk, n_ctx):
    nt = feat.shape[0]
    n_all = nt // SCAN_T
    wk = H_A * DK_A
    bwd = _bwd_chunk(n_ctx, n_all)

    def specs(chunk):
        return [pl.BlockSpec((SCAN_T, wk), lambda i: (chunk(i), 0)),
                pl.BlockSpec((SCAN_T, wk), lambda i: (chunk(i), 1)),
                pl.BlockSpec((SCAN_T, wk), lambda i: (chunk(i), 2)),
                pl.BlockSpec((SCAN_T, LANES), lambda i: (chunk(i), small_blk))]

    fwd = lambda i: i
    out = jax.ShapeDtypeStruct((nt, H_A * DV_A), F32)
    return pl.pallas_call(
        _gdn_body,
        out_shape=(out, out),
        grid=(n_all,),
        in_specs=specs(fwd) + specs(bwd) + [pl.BlockSpec((8, LANES), lambda i: (0, 0))],
        out_specs=(pl.BlockSpec((SCAN_T, H_A * DV_A), lambda i: (i, 0)),
                   pl.BlockSpec((SCAN_T, H_A * DV_A), lambda i: (bwd(i), 0))),
        scratch_shapes=[pltpu.VMEM((2 * H_A, DK_A, DV_A), F32)],
        compiler_params=_cparams(("arbitrary",)),
        name="gdn_scan",
    )(feat, feat, feat, p, feat, feat, feat, p, par)


def _gla_diag_terms(qi, ki, gi, reverse):
    n = SUB_T
    t3 = lax.broadcasted_iota(jnp.int32, (n, n, LANES), 0)
    s3 = lax.broadcasted_iota(jnp.int32, (n, n, LANES), 1)
    ok = (s3 >= t3) if reverse else (s3 <= t3)
    diff = gi[:, None, :] - gi[None, :, :]
    x = jnp.where(ok, jnp.exp(jnp.where(ok, diff, 0.0)) * qi[:, None, :] * ki[None, :, :], 0.0)
    return x.reshape(n * n, LANES).astype(BF16)


def _gla_body(qf, ff, vf, qb, fb, vb, lb_ref, of_ref, ob_ref, s_ref):
    @pl.when(pl.program_id(0) == 0)
    def _():
        s_ref[...] = jnp.zeros_like(s_ref)

    t_len, n = SCAN_T, SUB_T
    nb = t_len // n
    refs = ((qf, ff, vf, of_ref), (qb, fb, vb, ob_ref))
    cum = [jnp.where(_tri(t_len, d == 1, False), 1.0, 0.0).astype(BF16) for d in range(2)]
    ones = jnp.ones((LANES, LANES), BF16)
    row = lax.broadcasted_iota(jnp.int32, (n * n, LANES), 0)
    lane = lax.broadcasted_iota(jnp.int32, (n * n, LANES), 1)
    keep = (row % n) == lane
    sel = (lax.broadcasted_iota(jnp.int32, (n, n * n), 1) // n
           == lax.broadcasted_iota(jnp.int32, (n, n * n), 0))
    sel = jnp.where(sel, 1.0, 0.0).astype(BF16)
    st = []
    for d in range(2):
        for h in range(H_B):
            lo = h * DK_B
            fo = d * H_B * DK_B + lo
            fx = refs[d][1][:, fo:fo + DK_B]
            lb = lb_ref[d:d + 1, lo:lo + DK_B]
            f = lb + (1.0 - lb) * _sigmoid(fx)
            st.append(dict(d=d, lo=lo, c=d * H_B + h, log_f=jnp.log(jnp.maximum(f, F_TINY)),
                           kk=(1.0 - lb) * _sigmoid(-fx),
                           q=_silu(refs[d][0][:, lo:lo + DK_B]) * (DK_B ** -0.5),
                           v=refs[d][2][:, lo:lo + DV_B]))
    for x in st:
        x["gc"] = _cumdot(cum[x["d"]], x["log_f"])
        last_row = 0 if x["d"] == 1 else t_len - 1
        x["glast"] = x["gc"][last_row:last_row + 1, :]
        x["s_t"] = s_ref[x["c"]]
    inters = [_mm_nt(x["q"] * jnp.exp(x["gc"]), x["s_t"]) for x in st]
    for bi in range(nb):
        r0 = bi * n
        terms = [_gla_diag_terms(x["q"][r0:r0 + n], x["kk"][r0:r0 + n], x["gc"][r0:r0 + n], x["d"] == 1)
                 for x in st]
        sums = [jnp.dot(t, ones, preferred_element_type=F32) for t in terms]
        diags = [jnp.dot(sel, jnp.where(keep, r, 0.0).astype(BF16), preferred_element_type=F32) for r in sums]
        offs = []
        for x in st:
            reverse = x["d"] == 1
            if reverse and bi < nb - 1:
                p0, p1 = r0 + n, t_len
                ref = x["gc"][p0:p0 + 1, :]
            elif (not reverse) and bi > 0:
                p0, p1 = 0, r0
                ref = x["gc"][r0 - 1:r0, :]
            else:
                offs.append(None)
                continue
            qt = x["q"][r0:r0 + n] * jnp.exp(x["gc"][r0:r0 + n] - ref)
            kt = x["kk"][p0:p1] * jnp.exp(ref - x["gc"][p0:p1])
            offs.append((_mm_nt(qt, kt), p0, p1))
        for x, inter, diag, off in zip(st, inters, diags, offs):
            o_blk = inter[r0:r0 + n] + _mm(diag[:, :n], x["v"][r0:r0 + n])
            if off is not None:
                o_blk = o_blk + _mm(off[0], x["v"][off[1]:off[2]])
            refs[x["d"]][3][r0:r0 + n, x["lo"]:x["lo"] + DV_B] = o_blk
    upds = [_mm_tn(x["v"], x["kk"] * jnp.exp(x["glast"] - x["gc"])) for x in st]
    for x, upd in zip(st, upds):
        s_ref[x["c"]] = jnp.exp(x["glast"]) * x["s_t"] + upd


def _gla_scan(p, lb, q_blk, f_blk, v_blk, n_ctx):
    nt = p.shape[0]
    n_all = nt // SCAN_T
    wk = H_B * DK_B
    bwd = _bwd_chunk(n_ctx, n_all)

    def specs(chunk):
        return [pl.BlockSpec((SCAN_T, wk), lambda i: (chunk(i), q_blk)),
                pl.BlockSpec((SCAN_T, 2 * wk), lambda i: (chunk(i), f_blk)),
                pl.BlockSpec((SCAN_T, wk), lambda i: (chunk(i), v_blk))]

    fwd = lambda i: i
    out = jax.ShapeDtypeStruct((nt, H_B * DV_B), F32)
    return pl.pallas_call(
        _gla_body,
        out_shape=(out, out),
        grid=(n_all,),
        in_specs=specs(fwd) + specs(bwd) + [pl.BlockSpec((8, wk), lambda i: (0, 0))],
        out_specs=(pl.BlockSpec((SCAN_T, H_B * DV_B), lambda i: (i, 0)),
                   pl.BlockSpec((SCAN_T, H_B * DV_B), lambda i: (bwd(i), 0))),
        scratch_shapes=[pltpu.VMEM((2 * H_B, DV_B, DK_B), F32)],
        compiler_params=_cparams(("arbitrary",)),
        name="gla_scan",
    )(p, p, p, p, p, p, lb)


def _rope(x, cos, sin):
    parts = []
    for j in range(x.shape[1] // LANES):
        blk = x[:, j * LANES:(j + 1) * LANES]
        parts.append(blk * cos[:, j * LANES:(j + 1) * LANES]
                     + pltpu.roll(blk, LANES // 2, 1) * sin[:, j * LANES:(j + 1) * LANES])
    return jnp.concatenate(parts, axis=1)


def _ret_log_decay(d, h):
    e = (H_C - 1 - h) if d == 1 else h
    return math.log(1.0 - 2.0 ** (-5.0 - e))


def _ret_dir(q_ref, k_ref, v_ref, cos_ref, sin_ref, dec_ref, qs_ref, ks_ref, o_ref, s_ref, d):
    cos = cos_ref[...]
    sin = sin_ref[...]
    for h in range(H_C):
        c = d * H_C + h
        chunk_decay = math.exp(_ret_log_decay(d, h) * RET_T)
        q = _rope(q_ref[:, h * DK_C:(h + 1) * DK_C], cos, sin)
        k = _rope(k_ref[:, h * DK_C:(h + 1) * DK_C] * (DK_C ** -0.5), cos, sin)
        v = v_ref[:, h * DV_C:(h + 1) * DV_C]
        attn = _mm_nt(q, k) * dec_ref[c]
        qs = jnp.concatenate([qs_ref[c]] * (DK_C // LANES), axis=1)
        ks = jnp.concatenate([ks_ref[c]] * (DK_C // LANES), axis=1)
        s = s_ref[c]
        o_ref[:, h * DV_C:(h + 1) * DV_C] = _mm(q * qs, s) + _mm(attn, v)
        s_ref[c] = chunk_decay * s + _mm_tn(k * ks, v)


def _ret_body(qf, kf, vf, cf, sf, qb, kb, vb, cb, sb, dec_ref, qs_ref, ks_ref, of_ref, ob_ref, s_ref):
    @pl.when(pl.program_id(0) == 0)
    def _():
        s_ref[...] = jnp.zeros_like(s_ref)

    _ret_dir(qf, kf, vf, cf, sf, dec_ref, qs_ref, ks_ref, of_ref, s_ref, 0)
    _ret_dir(qb, kb, vb, cb, sb, dec_ref, qs_ref, ks_ref, ob_ref, s_ref, 1)


def _ret_tables():
    t = np.arange(RET_T, dtype=np.float64)
    dec, qs, ks = [], [], []
    for d in range(2):
        for h in range(H_C):
            lg = _ret_log_decay(d, h)
            pos = (RET_T - 1 - t) if d == 1 else t
            diff = pos[:, None] - pos[None, :]
            dec.append(np.where(diff >= 0, np.exp(lg * np.maximum(diff, 0.0)), 0.0))
            qs.append(np.broadcast_to(np.exp(lg * (pos + 1.0))[:, None], (RET_T, LANES)))
            ks.append(np.broadcast_to(np.exp(lg * (RET_T - 1.0 - pos))[:, None], (RET_T, LANES)))
    f = lambda xs: jnp.asarray(np.stack(xs), F32)
    return f(dec), f(qs), f(ks)


def _ret_scan(p, cos, sin, n_ctx):
    nt = p.shape[0]
    n_all = nt // RET_T
    wk, wv = H_C * DK_C, H_C * DV_C
    bwd = _bwd_chunk(n_ctx, n_all)
    dec, qs, ks = _ret_tables()

    def specs(chunk):
        return [pl.BlockSpec((RET_T, wk), lambda i: (chunk(i), 0)),
                pl.BlockSpec((RET_T, wk), lambda i: (chunk(i), 1)),
                pl.BlockSpec((RET_T, wv), lambda i: (chunk(i), 1)),
                pl.BlockSpec((RET_T, DK_C), lambda i: (chunk(i), 0)),
                pl.BlockSpec((RET_T, DK_C), lambda i: (chunk(i), 0))]

    full = lambda a: pl.BlockSpec(a.shape, lambda i: (0,) * a.ndim)
    fwd = lambda i: i
    out = jax.ShapeDtypeStruct((nt, wv), F32)
    return pl.pallas_call(
        _ret_body,
        out_shape=(out, out),
        grid=(n_all,),
        in_specs=specs(fwd) + specs(bwd) + [full(dec), full(qs), full(ks)],
        out_specs=(pl.BlockSpec((RET_T, wv), lambda i: (i, 0)),
                   pl.BlockSpec((RET_T, wv), lambda i: (bwd(i), 0))),
        scratch_shapes=[pltpu.VMEM((2 * H_C, DK_C, DV_C), F32)],
        compiler_params=_cparams(("arbitrary",)),
        name="ret_scan",
    )(p, p, p, cos, sin, p, p, p, cos, sin, dec, qs, ks)


def _outproj_body(*refs, groups, head_w):
    n_in = 3 * groups
    o_refs = refs[:n_in]
    gain_ref, w_ref, x_ref, mod_ref, g2_ref, r_ref = refs[n_in:n_in + 6]
    xo_ref, h_ref, lg_ref = refs[n_in + 6:]
    ys = []
    for gi in range(groups):
        of_ref, ob_ref, z_ref = o_refs[3 * gi:3 * gi + 3]
        wgrp = of_ref.shape[1]
        for h in range(wgrp // head_w):
            lo = h * head_w
            o = of_ref[:, lo:lo + head_w] + ob_ref[:, lo:lo + head_w]
            rms = o * lax.rsqrt(jnp.mean(o * o, axis=-1, keepdims=True) + EPS)
            gain = gain_ref[gi:gi + 1, 0:head_w]
            ys.append((rms * gain * _silu(z_ref[:, lo:lo + head_w])).astype(BF16))
    y = jnp.concatenate(ys, axis=1)
    m = mod_ref[0]
    x_new = x_ref[...] + m[2:3] * jnp.dot(y, w_ref[...], preferred_element_type=F32)
    xo_ref[...] = x_new
    h2 = _modulated(x_new, g2_ref[...], m[3:4], m[4:5])
    h_ref[...] = h2.astype(BF16)
    lg_ref[...] = _mm3(h2, r_ref[...])


def _outproj(o_groups, gains, w, x, mod, g2, router, head_w, n_ctx_tiles):
    nt, d = x.shape
    stream = _stream_of(n_ctx_tiles)
    in_specs, args = [], []
    for of, ob, z, zblk in o_groups:
        wg = of.shape[1]
        in_specs += [pl.BlockSpec((ROW_TILE, wg), lambda i: (i, 0)),
                     pl.BlockSpec((ROW_TILE, wg), lambda i: (i, 0)),
                     pl.BlockSpec((ROW_TILE, wg), lambda i, zblk=zblk: (i, zblk))]
        args += [of, ob, z]
    kdim = w.shape[0]
    in_specs += [pl.BlockSpec(gains.shape, lambda i: (0, 0)),
                 pl.BlockSpec((kdim, d), lambda i: (0, 0)),
                 pl.BlockSpec((ROW_TILE, d), lambda i: (i, 0)),
                 pl.BlockSpec((1, 8, d), lambda i: (stream(i), 0, 0)),
                 pl.BlockSpec((1, d), lambda i: (0, 0)),
                 pl.BlockSpec((d, LANES), lambda i: (0, 0))]
    args += [gains, w, x, mod, g2, router]
    row = lambda n: pl.BlockSpec((ROW_TILE, n), lambda i: (i, 0))
    return pl.pallas_call(
        functools.partial(_outproj_body, groups=len(o_groups), head_w=head_w),
        out_shape=(jax.ShapeDtypeStruct((nt, d), F32), jax.ShapeDtypeStruct((nt, d), BF16),
                   jax.ShapeDtypeStruct((nt, LANES), F32)),
        grid=(nt // ROW_TILE,),
        in_specs=in_specs,
        out_specs=(row(d), row(d), row(LANES)),
        compiler_params=_cparams(("parallel",)),
        name="out_proj",
    )(*args)


def _ffn_rows(cap):
    pack = 2 * SUBLANES
    return max(r for r in range(pack, FFN_ROWS + 1, pack) if cap % r == 0)


def _ffn_body(xs_ref, gate_ref, wg_ref, wu_ref, wd_ref, o_ref):
    f = pl.program_id(1)
    wg = wg_ref[0, 0].astype(BF16)
    wu = wu_ref[0, 0].astype(BF16)
    wd = wd_ref[0, 0].astype(BF16)
    cap = xs_ref.shape[1]
    rows = _ffn_rows(cap)

    @pl.when(f == 0)
    def _():
        o_ref[...] = jnp.zeros_like(o_ref)

    hids = []
    for r0 in range(0, cap, rows):
        xs = xs_ref[0, r0:r0 + rows, :]
        a = jnp.dot(xs, wg, preferred_element_type=F32)
        b = jnp.dot(xs, wu, preferred_element_type=F32)
        hids.append((_silu(a) * b).astype(BF16))
    for r0, hid in zip(range(0, cap, rows), hids):
        o_ref[0, r0:r0 + rows, :] += jnp.dot(hid, wd, preferred_element_type=F32)

    @pl.when(f == pl.num_programs(1) - 1)
    def _():
        o_ref[0] = o_ref[0] * gate_ref[0]


def _expert_ffn(xs, gate, w_gate, w_up, w_down, layer):
    e, cap, d = xs.shape
    dff = w_gate.shape[3]
    return pl.pallas_call(
        _ffn_body,
        out_shape=jax.ShapeDtypeStruct((e, cap, d), F32),
        grid=(e, dff // FFN_TILE),
        in_specs=[pl.BlockSpec((1, cap, d), lambda i, f: (i, 0, 0)),
                  pl.BlockSpec((1, cap, 1), lambda i, f: (i, 0, 0)),
                  pl.BlockSpec((1, 1, d, FFN_TILE), lambda i, f: (layer, i, 0, f)),
                  pl.BlockSpec((1, 1, d, FFN_TILE), lambda i, f: (layer, i, 0, f)),
                  pl.BlockSpec((1, 1, FFN_TILE, d), lambda i, f: (layer, i, f, 0))],
        out_specs=pl.BlockSpec((1, cap, d), lambda i, f: (i, 0, 0)),
        compiler_params=_cparams(("parallel", "arbitrary")),
        name="expert_ffn",
    )(xs, gate, w_gate, w_up, w_down)


def _final_body(x_ref, g_ref, o_ref):
    x = x_ref[...]
    o_ref[...] = x * lax.rsqrt(jnp.mean(x * x, axis=-1, keepdims=True) + EPS) * g_ref[...]


def _final_norm(x, g):
    nt, d = x.shape
    return pl.pallas_call(
        _final_body,
        out_shape=jax.ShapeDtypeStruct((nt, d), F32),
        grid=(nt // ROW_TILE,),
        in_specs=[pl.BlockSpec((ROW_TILE, d), lambda i: (i, 0)), pl.BlockSpec((1, d), lambda i: (0, 0))],
        out_specs=pl.BlockSpec((ROW_TILE, d), lambda i: (i, 0)),
        compiler_params=_cparams(("parallel",)),
        name="final_norm",
    )(x, g)


def _route(logits, cap):
    aff = jax.nn.softmax(logits, axis=-1)
    return lax.top_k(aff.T, cap)


def _moe(h2, logits, n_ctx_rows, w_gate, w_up, w_down, layer, with_ctx):
    nt, d = h2.shape
    n_lat = nt - n_ctx_rows
    lg = logits[:, :N_EXPERTS]
    cap_l = EC_CAPACITY * n_lat // N_EXPERTS
    gate_l, idx_l = _route(lg[n_ctx_rows:], cap_l)
    idx_l = idx_l + n_ctx_rows
    if with_ctx:
        cap_c = EC_CAPACITY * n_ctx_rows // N_EXPERTS
        gate_c, idx_c = _route(lg[:n_ctx_rows], cap_c)
        gate = jnp.concatenate([gate_l, gate_c], axis=1)
        idx = jnp.concatenate([idx_l, idx_c], axis=1)
    else:
        gate, idx = gate_l, idx_l
    xs = h2[idx]
    ys = _expert_ffn(xs, gate[..., None], w_gate, w_up, w_down, layer)
    return jnp.zeros((nt, d), F32).at[idx.reshape(-1)].add(ys.reshape(-1, d))


def _rope_tables(n_ctx_rows, n_lat):
    quarter = DK_C // 4
    inv = ROPE_BASE ** (-jnp.arange(quarter, dtype=F32) / quarter)
    t = jnp.arange(n_lat)
    ang_r = (t // GRID_W).astype(F32)[:, None] * inv[None, :]
    ang_c = (t % GRID_W).astype(F32)[:, None] * inv[None, :]
    cos = jnp.concatenate([jnp.cos(ang_r)] * 2 + [jnp.cos(ang_c)] * 2, axis=1)
    sin = jnp.concatenate([-jnp.sin(ang_r), jnp.sin(ang_r), -jnp.sin(ang_c), jnp.sin(ang_c)], axis=1)
    cos = jnp.concatenate([jnp.ones((n_ctx_rows, DK_C), F32), cos], axis=0)
    sin = jnp.concatenate([jnp.zeros((n_ctx_rows, DK_C), F32), sin], axis=0)
    return cos, sin


def _pad_cols(a, n):
    return jnp.pad(a, ((0, 0), (0, n - a.shape[1])))


def kernel(x, c, ctx, c_ctx, ada_w, ada_b, norm_g, final_g, ev_w_in, ev_conv, ev_a_log, ev_dt_bias, ev_gdn_norm, ev_hgrn_lb, ev_hgrn_norm, ev_w_out, od_w_in, od_w_out, moe_router, moe_w_gate, moe_w_up, moe_w_down):
    depth = ada_w.shape[0]
    n_lat, d = x.shape[1], x.shape[2]
    n_ctx_rows = ctx.shape[1]
    assert x.shape[0] == 1 and n_ctx_rows % RET_T == 0 and n_lat % RET_T == 0 and RET_T == ROW_TILE
    n_ctx_tiles = n_ctx_rows // ROW_TILE

    xs = jnp.concatenate([ctx[0], x[0]], axis=0)
    cond = jnp.zeros((8, d), F32).at[0].set(jax.nn.silu(c_ctx)).at[1].set(jax.nn.silu(c[0]))
    p_l = jax.nn.softmax(ev_hgrn_lb.astype(F32), axis=1)
    lbs = jnp.cumsum(p_l, axis=1) - p_l[:, :1]
    cos, sin = _rope_tables(n_ctx_rows, n_lat)
    router = jnp.pad(moe_router, ((0, 0), (0, 0), (0, LANES - N_EXPERTS)))

    wka, wva, wkb, wvb = H_A * DK_A, H_A * DV_A, H_B * DK_B, H_B * DV_B
    n_small = 4 * H_A
    big = 2 * wka + 2 * wva + 3 * wkb + 2 * wvb
    small_at = 2 * wka + 2 * wva

    for l in range(depth):
        last = l == depth - 1
        i = l // 2
        mod = _dense(cond, ada_w[l].astype(BF16), ada_b[l][None, :], 6 * d // 4)
        mod = jnp.pad(mod[:2].reshape(2, 6, d), ((0, 0), (0, 2), (0, 0)))
        g1 = norm_g[l, 0][None, :]
        g2 = norm_g[l, 1][None, :]
        if l % 2 == 0:
            w_in = ev_w_in[i]
            qb_at = small_at + n_small
            fb_at = qb_at + wkb
            ib_at = fb_at + 2 * wkb
            w_perm = jnp.concatenate([w_in[:, :small_at], w_in[:, fb_at:ib_at], w_in[:, qb_at:fb_at],
                                      w_in[:, ib_at:],
                                      _pad_cols(w_in[:, small_at:small_at + n_small], LANES)], axis=1)
            p = _proj(xs, g1, mod, w_perm.astype(BF16), n_ctx_tiles)
            feat = _even_feat(p, jnp.pad(ev_conv[i], ((0, 8 - CONV_K), (0, 0))), n_ctx_tiles)
            par = jnp.zeros((8, LANES), F32)
            par = par.at[0, :2 * H_A].set(-jnp.exp(ev_a_log[i].astype(F32)).reshape(-1))
            par = par.at[1, :2 * H_A].set(ev_dt_bias[i].astype(F32).reshape(-1))
            oa_f, oa_b = _gdn_scan(feat, p, par, big // LANES, n_ctx_rows // SCAN_T)
            lb = jnp.pad(lbs[:, i], ((0, 6), (0, 0)))
            ob_f, ob_b = _gla_scan(p, lb, 6, 2, 7, n_ctx_rows // SCAN_T)
            gains = jnp.zeros((8, LANES), F32).at[0].set(ev_gdn_norm[i]).at[1].set(ev_hgrn_norm[i])
            groups = [(oa_f, oa_b, p, 3), (ob_f, ob_b, p, 8)]
            xs, h2, logits = _outproj(groups, gains, ev_w_out[i].astype(BF16), xs, mod, g2,
                                      router[l], DV_A, n_ctx_tiles)
        else:
            p = _proj(xs, g1, mod, od_w_in[i].astype(BF16), n_ctx_tiles)
            o_f, o_b = _ret_scan(p, cos, sin, n_ctx_rows // RET_T)
            gains = jnp.ones((8, DV_C), F32)
            groups = [(o_f, o_b, p, 2)]
            xs, h2, logits = _outproj(groups, gains, od_w_out[i].astype(BF16), xs, mod, g2,
                                      router[l], DV_C, n_ctx_tiles)
        moe = _moe(h2, logits, n_ctx_rows, moe_w_gate, moe_w_up, moe_w_down, l, not last)
        stream_gate = jnp.concatenate([jnp.broadcast_to(mod[0, 5], (n_ctx_rows, d)),
                                       jnp.broadcast_to(mod[1, 5], (n_lat, d))], axis=0)
        xs = xs + stream_gate * moe
    out = _final_norm(xs, final_g[None, :])
    return out[n_ctx_rows:][None]
```

```python
import functools
import math

import numpy as np
import jax
import jax.numpy as jnp
from jax import lax
from jax.experimental import pallas as pl
from jax.experimental.pallas import tpu as pltpu

F32 = jnp.float32
BF16 = jnp.bfloat16

GRID_W = 64
EPS = 1e-6
F_TINY = 1e-30
H_A, DK_A, DV_A = 4, 128, 128
CONV_K = 5
H_B, DK_B, DV_B = 4, 128, 128
H_C, DK_C, DV_C = 4, 256, 512
ROPE_BASE = 10000.0
N_EXPERTS = 16
EC_CAPACITY = 2

LANES = 128
SUBLANES = 8
ROW_TILE = 256
SCAN_T = 64
SUB_T = 16
RET_T = 256
FFN_TILE = 256
FFN_ROWS = 512
COMBINE_WIN = 80
VMEM_LIMIT = 56 * 1024 * 1024


def _cparams(sem):
    return pltpu.CompilerParams(dimension_semantics=sem, vmem_limit_bytes=VMEM_LIMIT)


def _mm(a, b):
    return jnp.dot(a.astype(BF16), b.astype(BF16), preferred_element_type=F32)


def _mm_nt(a, b):
    return lax.dot_general(a.astype(BF16), b.astype(BF16), (((1,), (1,)), ((), ())),
                           preferred_element_type=F32)


def _mm_tn(a, b):
    return lax.dot_general(a.astype(BF16), b.astype(BF16), (((0,), (0,)), ((), ())),
                           preferred_element_type=F32)


def _split2(x):
    hi = x.astype(BF16)
    return hi, (x - hi.astype(F32)).astype(BF16)


def _mm3(a, b):
    ah, al = _split2(a)
    bh, bl = _split2(b)
    dot = lambda x, y: jnp.dot(x, y, preferred_element_type=F32)
    return dot(ah, bh) + (dot(ah, bl) + dot(al, bh))


def _cumdot(m, g):
    w = g.shape[1]
    hi = g.astype(BF16)
    r1 = g - hi.astype(F32)
    mid = r1.astype(BF16)
    lo = (r1 - mid.astype(F32)).astype(BF16)
    out = jnp.dot(m, jnp.concatenate([hi, mid, lo], axis=1), preferred_element_type=F32)
    return out[:, :w] + (out[:, w:2 * w] + out[:, 2 * w:])


def _sigmoid(x):
    return 1.0 / (1.0 + jnp.exp(-x))


def _silu(x):
    return x * _sigmoid(x)


def _softplus(x):
    return jnp.maximum(x, 0.0) + jnp.log(1.0 + jnp.exp(-jnp.abs(x)))


def _modulated(x, g, shift, scale):
    ms = jnp.mean(x * x, axis=-1, keepdims=True)
    return x * lax.rsqrt(ms + EPS) * g * (1.0 + scale) + shift


def _tri(n, reverse, strict):
    t = lax.broadcasted_iota(jnp.int32, (n, n), 0)
    s = lax.broadcasted_iota(jnp.int32, (n, n), 1)
    if reverse:
        return (s > t) if strict else (s >= t)
    return (s < t) if strict else (s <= t)


def _stream_of(n_ctx_tiles):
    return lambda i: jnp.where(i >= n_ctx_tiles, 1, 0)


def _bwd_chunk(n_ctx, n_all):
    return lambda i: jnp.where(i < n_ctx, n_ctx - 1 - i, n_all + n_ctx - 1 - i)


def _dense_body(a_ref, w_ref, b_ref, o_ref):
    o_ref[...] = _mm(a_ref[...], w_ref[...]) + b_ref[...]


def _dense(a, w, b, tn):
    m, k = a.shape
    n = w.shape[1]
    return pl.pallas_call(
        _dense_body,
        out_shape=jax.ShapeDtypeStruct((m, n), F32),
        grid=(n // tn,),
        in_specs=[pl.BlockSpec((m, k), lambda j: (0, 0)),
                  pl.BlockSpec((k, tn), lambda j: (0, j)),
                  pl.BlockSpec((1, tn), lambda j: (0, j))],
        out_specs=pl.BlockSpec((m, tn), lambda j: (0, j)),
        compiler_params=_cparams(("arbitrary",)),
        name="ada_dense",
    )(a, w, b)


def _proj_body(x_ref, g_ref, mod_ref, w_ref, o_ref):
    m = mod_ref[0]
    h = _modulated(x_ref[...], g_ref[...], m[0:1], m[1:2])
    o_ref[...] = jnp.dot(h.astype(BF16), w_ref[...], preferred_element_type=F32)


def _proj(x, g, mod, w, n_ctx_tiles):
    nt, d = x.shape
    n = w.shape[1]
    stream = _stream_of(n_ctx_tiles)
    return pl.pallas_call(
        _proj_body,
        out_shape=jax.ShapeDtypeStruct((nt, n), F32),
        grid=(nt // ROW_TILE,),
        in_specs=[pl.BlockSpec((ROW_TILE, d), lambda i: (i, 0)),
                  pl.BlockSpec((1, d), lambda i: (0, 0)),
                  pl.BlockSpec((1, 8, d), lambda i: (stream(i), 0, 0)),
                  pl.BlockSpec((d, n), lambda i: (0, 0))],
        out_specs=pl.BlockSpec((ROW_TILE, n), lambda i: (i, 0)),
        compiler_params=_cparams(("parallel",)),
        name="in_proj",
    )(x, g, mod, w)


def _feat_body(cur_ref, prev_ref, next_ref, w_ref, o_ref, ext_ref, *, n_ctx_tiles, n_tiles):
    i = pl.program_id(0)
    halo = SUBLANES
    prev_ok = jnp.logical_and(i != 0, i != n_ctx_tiles)
    next_ok = jnp.logical_and(i != n_ctx_tiles - 1, i != n_tiles - 1)
    ext_ref[0:halo, :] = jnp.where(prev_ok, prev_ref[...], 0.0)
    ext_ref[halo:halo + ROW_TILE, :] = cur_ref[...]
    ext_ref[halo + ROW_TILE:, :] = jnp.where(next_ok, next_ref[...], 0.0)
    acc = None
    for j in range(CONV_K):
        start = halo - CONV_K // 2 + j
        term = ext_ref[start:start + ROW_TILE, :] * w_ref[j:j + 1, :]
        acc = term if acc is None else acc + term
    y = _silu(acc)
    wq = H_A * DK_A
    for h in range(2 * H_A):
        lo = h * DK_A
        blk = y[:, lo:lo + DK_A]
        nrm = blk * lax.rsqrt(jnp.sum(blk * blk, axis=-1, keepdims=True) + EPS)
        if lo < wq:
            nrm = nrm * (DK_A ** -0.5)
        o_ref[:, lo:lo + DK_A] = nrm
    o_ref[:, 2 * wq:] = y[:, 2 * wq:]


def _even_feat(p, conv_w, n_ctx_tiles):
    nt = p.shape[0]
    wc = conv_w.shape[1]
    n_tiles = nt // ROW_TILE
    per = ROW_TILE // SUBLANES
    last_blk = nt // SUBLANES - 1
    return pl.pallas_call(
        functools.partial(_feat_body, n_ctx_tiles=n_ctx_tiles, n_tiles=n_tiles),
        out_shape=jax.ShapeDtypeStruct((nt, wc), F32),
        grid=(n_tiles,),
        in_specs=[pl.BlockSpec((ROW_TILE, wc), lambda i: (i, 0)),
                  pl.BlockSpec((SUBLANES, wc), lambda i: (jnp.maximum(i * per - 1, 0), 0)),
                  pl.BlockSpec((SUBLANES, wc), lambda i: (jnp.minimum((i + 1) * per, last_blk), 0)),
                  pl.BlockSpec((8, wc), lambda i: (0, 0))],
        out_specs=pl.BlockSpec((ROW_TILE, wc), lambda i: (i, 0)),
        scratch_shapes=[pltpu.VMEM((ROW_TILE + 2 * SUBLANES, wc), F32)],
        compiler_params=_cparams(("parallel",)),
        name="even_feat",
    )(p, p, p, conv_w)


def _unit_tri_inverses(mats, n):
    r = lax.broadcasted_iota(jnp.int32, (n, n), 0)
    c = lax.broadcasted_iota(jnp.int32, (n, n), 1)
    eye = jnp.where(r == c, 1.0, 0.0)
    ps = [eye - a for a in mats]
    aks = [_mm3(a, a) for a in mats]
    for _ in range(int(math.log2(n)) - 2):
        outs = [_mm3(jnp.concatenate([ak, p], axis=0), ak) for ak, p in zip(aks, ps)]
        aks = [o[:n] for o in outs]
        ps = [p + o[n:] for p, o in zip(ps, outs)]
    return [p + _mm3(p, ak) for p, ak in zip(ps, aks)]


def _gdn_body(qf, kf, vf, smf, qb, kb, vb, smb, par_ref, of_ref, ob_ref, s_ref):
    @pl.when(pl.program_id(0) == 0)
    def _():
        s_ref[...] = jnp.zeros_like(s_ref)

    t_len = SCAN_T
    chains = [(d, h) for d in range(2) for h in range(H_A)]
    refs = ((qf, kf, vf, smf, of_ref), (qb, kb, vb, smb, ob_ref))
    incl = [_tri(t_len, d == 1, False) for d in range(2)]
    strict = [_tri(t_len, d == 1, True) for d in range(2)]
    gcs, gcts, betas = [], [], []
    for d in range(2):
        sm = refs[d][3][...]
        g_all = par_ref[0:1, :] * _softplus(sm + par_ref[1:2, :])
        betas.append(_sigmoid(sm))
        gc = _cumdot(jnp.where(incl[d], 1.0, 0.0).astype(BF16), g_all)
        gcs.append(gc)
        gcts.append(gc.T)
    st = []
    for d, h in chains:
        c = d * H_A + h
        lo = h * DK_A
        gcol = gcs[d][:, c:c + 1]
        grow = gcts[d][c:c + 1, :]
        dec = jnp.where(incl[d], jnp.exp(jnp.where(incl[d], gcol - grow, 0.0)), 0.0)
        beta = betas[d][:, 2 * H_A + c:2 * H_A + c + 1]
        q = refs[d][0][:, lo:lo + DK_A]
        k = refs[d][1][:, lo:lo + DK_A]
        v = refs[d][2][:, lo:lo + DK_A]
        last_row = 0 if d == 1 else t_len - 1
        glast = gcs[d][last_row:last_row + 1, c:c + 1]
        st.append(dict(c=c, lo=lo, d=d, dec=dec, beta=beta, q=q, k=k, v=v, kb=k * beta, eg=jnp.exp(gcol),
                       kt=k * jnp.exp(glast - gcol), cd=jnp.exp(glast)))
    kks = [_mm_nt(x["kb"], x["k"]) for x in st]
    qks = [_mm_nt(x["q"], x["k"]) for x in st]
    invs = _unit_tri_inverses([jnp.where(strict[x["d"]], kk * x["dec"], 0.0) for x, kk in zip(st, kks)], t_len)
    uws = [_mm3(inv, jnp.concatenate([x["v"] * x["beta"], x["kb"] * x["eg"]], axis=1)) for x, inv in zip(st, invs)]
    ss = [s_ref[x["c"]] for x in st]
    vns = [uw[:, :DV_A] - _mm(uw[:, DV_A:], s) for uw, s in zip(uws, ss)]
    inters = [_mm(x["q"] * x["eg"], s) for x, s in zip(st, ss)]
    intras = [_mm(qk * x["dec"], vn) for x, qk, vn in zip(st, qks, vns)]
    upds = [_mm_tn(x["kt"], vn) for x, vn in zip(st, vns)]
    for x, s, inter, intra, upd in zip(st, ss, inters, intras, upds):
        refs[x["d"]][4][:, x["lo"]:x["lo"] + DV_A] = inter + intra
        s_ref[x["c"]] = x["cd"] * s + upd


def _gdn_scan(feat, p, par, small_blk, n_ctx):
    nt = feat.shape[0]
    n_all = nt // SCAN_T
    wk = H_A * DK_A
    bwd = _bwd_chunk(n_ctx, n_all)

    def specs(chunk):
        return [pl.BlockSpec((SCAN_T, wk), lambda i: (chunk(i), 0)),
                pl.BlockSpec((SCAN_T, wk), lambda i: (chunk(i), 1)),
                pl.BlockSpec((SCAN_T, wk), lambda i: (chunk(i), 2)),
                pl.BlockSpec((SCAN_T, LANES), lambda i: (chunk(i), small_blk))]

    fwd = lambda i: i
    out = jax.ShapeDtypeStruct((nt, H_A * DV_A), F32)
    return pl.pallas_call(
        _gdn_body,
        out_shape=(out, out),
        grid=(n_all,),
        in_specs=specs(fwd) + specs(bwd) + [pl.BlockSpec((8, LANES), lambda i: (0, 0))],
        out_specs=(pl.BlockSpec((SCAN_T, H_A * DV_A), lambda i: (i, 0)),
                   pl.BlockSpec((SCAN_T, H_A * DV_A), lambda i: (bwd(i), 0))),
        scratch_shapes=[pltpu.VMEM((2 * H_A, DK_A, DV_A), F32)],
        compiler_params=_cparams(("arbitrary",)),
        name="gdn_scan",
    )(feat, feat, feat, p, feat, feat, feat, p, par)


def _gla_diag_terms(qi, ki, gi, reverse):
    n = SUB_T
    t3 = lax.broadcasted_iota(jnp.int32, (n, n, LANES), 0)
    s3 = lax.broadcasted_iota(jnp.int32, (n, n, LANES), 1)
    ok = (s3 >= t3) if reverse else (s3 <= t3)
    diff = gi[:, None, :] - gi[None, :, :]
    x = jnp.where(ok, jnp.exp(jnp.where(ok, diff, 0.0)) * qi[:, None, :] * ki[None, :, :], 0.0)
    return x.reshape(n * n, LANES).astype(BF16)


def _gla_body(qf, ff, vf, qb, fb, vb, lb_ref, of_ref, ob_ref, s_ref):
    @pl.when(pl.program_id(0) == 0)
    def _():
        s_ref[...] = jnp.zeros_like(s_ref)

    t_len, n = SCAN_T, SUB_T
    nb = t_len // n
    refs = ((qf, ff, vf, of_ref), (qb, fb, vb, ob_ref))
    cum = [jnp.where(_tri(t_len, d == 1, False), 1.0, 0.0).astype(BF16) for d in range(2)]
    ones = jnp.ones((LANES, LANES), BF16)
    row = lax.broadcasted_iota(jnp.int32, (n * n, LANES), 0)
    lane = lax.broadcasted_iota(jnp.int32, (n * n, LANES), 1)
    keep = (row % n) == lane
    sel = (lax.broadcasted_iota(jnp.int32, (n, n * n), 1) // n
           == lax.broadcasted_iota(jnp.int32, (n, n * n), 0))
    sel = jnp.where(sel, 1.0, 0.0).astype(BF16)
    st = []
    for d in range(2):
        for h in range(H_B):
            lo = h * DK_B
            fo = d * H_B * DK_B + lo
            fx = refs[d][1][:, fo:fo + DK_B]
            lb = lb_ref[d:d + 1, lo:lo + DK_B]
            f = lb + (1.0 - lb) * _sigmoid(fx)
            st.append(dict(d=d, lo=lo, c=d * H_B + h, log_f=jnp.log(jnp.maximum(f, F_TINY)),
                           kk=(1.0 - lb) * _sigmoid(-fx),
                           q=_silu(refs[d][0][:, lo:lo + DK_B]) * (DK_B ** -0.5),
                           v=refs[d][2][:, lo:lo + DV_B]))
    for x in st:
        x["gc"] = _cumdot(cum[x["d"]], x["log_f"])
        last_row = 0 if x["d"] == 1 else t_len - 1
        x["glast"] = x["gc"][last_row:last_row + 1, :]
        x["s_t"] = s_ref[x["c"]]
    inters = [_mm_nt(x["q"] * jnp.exp(x["gc"]), x["s_t"]) for x in st]
    for bi in range(nb):
        r0 = bi * n
        terms = [_gla_diag_terms(x["q"][r0:r0 + n], x["kk"][r0:r0 + n], x["gc"][r0:r0 + n], x["d"] == 1)
                 for x in st]
        sums = [jnp.dot(t, ones, preferred_element_type=F32) for t in terms]
        diags = [jnp.dot(sel, jnp.where(keep, r, 0.0).astype(BF16), preferred_element_type=F32) for r in sums]
        offs = []
        for x in st:
            reverse = x["d"] == 1
            if reverse and bi < nb - 1:
                p0, p1 = r0 + n, t_len
                ref = x["gc"][p0:p0 + 1, :]
            elif (not reverse) and bi > 0:
                p0, p1 = 0, r0
                ref = x["gc"][r0 - 1:r0, :]
            else:
                offs.append(None)
                continue
            qt = x["q"][r0:r0 + n] * jnp.exp(x["gc"][r0:r0 + n] - ref)
            kt = x["kk"][p0:p1] * jnp.exp(ref - x["gc"][p0:p1])
            offs.append((_mm_nt(qt, kt), p0, p1))
        for x, inter, diag, off in zip(st, inters, diags, offs):
            o_blk = inter[r0:r0 + n] + _mm(diag[:, :n], x["v"][r0:r0 + n])
            if off is not None:
                o_blk = o_blk + _mm(off[0], x["v"][off[1]:off[2]])
            refs[x["d"]][3][r0:r0 + n, x["lo"]:x["lo"] + DV_B] = o_blk
    upds = [_mm_tn(x["v"], x["kk"] * jnp.exp(x["glast"] - x["gc"])) for x in st]
    for x, upd in zip(st, upds):
        s_ref[x["c"]] = jnp.exp(x["glast"]) * x["s_t"] + upd


def _gla_scan(p, lb, q_blk, f_blk, v_blk, n_ctx):
    nt = p.shape[0]
    n_all = nt // SCAN_T
    wk = H_B * DK_B
    bwd = _bwd_chunk(n_ctx, n_all)

    def specs(chunk):
        return [pl.BlockSpec((SCAN_T, wk), lambda i: (chunk(i), q_blk)),
                pl.BlockSpec((SCAN_T, 2 * wk), lambda i: (chunk(i), f_blk)),
                pl.BlockSpec((SCAN_T, wk), lambda i: (chunk(i), v_blk))]

    fwd = lambda i: i
    out = jax.ShapeDtypeStruct((nt, H_B * DV_B), F32)
    return pl.pallas_call(
        _gla_body,
        out_shape=(out, out),
        grid=(n_all,),
        in_specs=specs(fwd) + specs(bwd) + [pl.BlockSpec((8, wk), lambda i: (0, 0))],
        out_specs=(pl.BlockSpec((SCAN_T, H_B * DV_B), lambda i: (i, 0)),
                   pl.BlockSpec((SCAN_T, H_B * DV_B), lambda i: (bwd(i), 0))),
        scratch_shapes=[pltpu.VMEM((2 * H_B, DV_B, DK_B), F32)],
        compiler_params=_cparams(("arbitrary",)),
        name="gla_scan",
    )(p, p, p, p, p, p, lb)


def _rope(x, cos, sin):
    parts = []
    for j in range(x.shape[1] // LANES):
        blk = x[:, j * LANES:(j + 1) * LANES]
        parts.append(blk * cos[:, j * LANES:(j + 1) * LANES]
                     + pltpu.roll(blk, LANES // 2, 1) * sin[:, j * LANES:(j + 1) * LANES])
    return jnp.concatenate(parts, axis=1)


def _ret_log_decay(d, h):
    e = (H_C - 1 - h) if d == 1 else h
    return math.log(1.0 - 2.0 ** (-5.0 - e))


def _ret_dir(q_ref, k_ref, v_ref, cos_ref, sin_ref, dec_ref, qs_ref, ks_ref, o_ref, s_ref, d):
    cos = cos_ref[...]
    sin = sin_ref[...]
    for h in range(H_C):
        c = d * H_C + h
        chunk_decay = math.exp(_ret_log_decay(d, h) * RET_T)
        q = _rope(q_ref[:, h * DK_C:(h + 1) * DK_C], cos, sin)
        k = _rope(k_ref[:, h * DK_C:(h + 1) * DK_C] * (DK_C ** -0.5), cos, sin)
        v = v_ref[:, h * DV_C:(h + 1) * DV_C]
        attn = _mm_nt(q, k) * dec_ref[c]
        qs = jnp.concatenate([qs_ref[c]] * (DK_C // LANES), axis=1)
        ks = jnp.concatenate([ks_ref[c]] * (DK_C // LANES), axis=1)
        s = s_ref[c]
        o_ref[:, h * DV_C:(h + 1) * DV_C] = _mm(q * qs, s) + _mm(attn, v)
        s_ref[c] = chunk_decay * s + _mm_tn(k * ks, v)


def _ret_body(qf, kf, vf, cf, sf, qb, kb, vb, cb, sb, dec_ref, qs_ref, ks_ref, of_ref, ob_ref, s_ref):
    @pl.when(pl.program_id(0) == 0)
    def _():
        s_ref[...] = jnp.zeros_like(s_ref)

    _ret_dir(qf, kf, vf, cf, sf, dec_ref, qs_ref, ks_ref, of_ref, s_ref, 0)
    _ret_dir(qb, kb, vb, cb, sb, dec_ref, qs_ref, ks_ref, ob_ref, s_ref, 1)


def _ret_tables():
    t = np.arange(RET_T, dtype=np.float64)
    dec, qs, ks = [], [], []
    for d in range(2):
        for h in range(H_C):
            lg = _ret_log_decay(d, h)
            pos = (RET_T - 1 - t) if d == 1 else t
            diff = pos[:, None] - pos[None, :]
            dec.append(np.where(diff >= 0, np.exp(lg * np.maximum(diff, 0.0)), 0.0))
            qs.append(np.broadcast_to(np.exp(lg * (pos + 1.0))[:, None], (RET_T, LANES)))
            ks.append(np.broadcast_to(np.exp(lg * (RET_T - 1.0 - pos))[:, None], (RET_T, LANES)))
    f = lambda xs: jnp.asarray(np.stack(xs), F32)
    return f(dec), f(qs), f(ks)


def _ret_scan(p, cos, sin, n_ctx):
    nt = p.shape[0]
    n_all = nt // RET_T
    wk, wv = H_C * DK_C, H_C * DV_C
    bwd = _bwd_chunk(n_ctx, n_all)
    dec, qs, ks = _ret_tables()

    def specs(chunk):
        return [pl.BlockSpec((RET_T, wk), lambda i: (chunk(i), 0)),
                pl.BlockSpec((RET_T, wk), lambda i: (chunk(i), 1)),
                pl.BlockSpec((RET_T, wv), lambda i: (chunk(i), 1)),
                pl.BlockSpec((RET_T, DK_C), lambda i: (chunk(i), 0)),
                pl.BlockSpec((RET_T, DK_C), lambda i: (chunk(i), 0))]

    full = lambda a: pl.BlockSpec(a.shape, lambda i: (0,) * a.ndim)
    fwd = lambda i: i
    out = jax.ShapeDtypeStruct((nt, wv), F32)
    return pl.pallas_call(
        _ret_body,
        out_shape=(out, out),
        grid=(n_all,),
        in_specs=specs(fwd) + specs(bwd) + [full(dec), full(qs), full(ks)],
        out_specs=(pl.BlockSpec((RET_T, wv), lambda i: (i, 0)),
                   pl.BlockSpec((RET_T, wv), lambda i: (bwd(i), 0))),
        scratch_shapes=[pltpu.VMEM((2 * H_C, DK_C, DV_C), F32)],
        compiler_params=_cparams(("arbitrary",)),
        name="ret_scan",
    )(p, p, p, cos, sin, p, p, p, cos, sin, dec, qs, ks)


def _outproj_body(*refs, groups, head_w):
    n_in = 3 * groups
    o_refs = refs[:n_in]
    gain_ref, w_ref, x_ref, mod_ref, g2_ref, r_ref = refs[n_in:n_in + 6]
    xo_ref, h_ref, lg_ref = refs[n_in + 6:]
    ys = []
    for gi in range(groups):
        of_ref, ob_ref, z_ref = o_refs[3 * gi:3 * gi + 3]
        wgrp = of_ref.shape[1]
        for h in range(wgrp // head_w):
            lo = h * head_w
            o = of_ref[:, lo:lo + head_w] + ob_ref[:, lo:lo + head_w]
            rms = o * lax.rsqrt(jnp.mean(o * o, axis=-1, keepdims=True) + EPS)
            gain = gain_ref[gi:gi + 1, 0:head_w]
            ys.append((rms * gain * _silu(z_ref[:, lo:lo + head_w])).astype(BF16))
    y = jnp.concatenate(ys, axis=1)
    m = mod_ref[0]
    x_new = x_ref[...] + m[2:3] * jnp.dot(y, w_ref[...], preferred_element_type=F32)
    xo_ref[...] = x_new
    h2 = _modulated(x_new, g2_ref[...], m[3:4], m[4:5])
    h_ref[...] = h2.astype(BF16)
    lg_ref[...] = _mm3(h2, r_ref[...])


def _outproj(o_groups, gains, w, x, mod, g2, router, head_w, n_ctx_tiles):
    nt, d = x.shape
    stream = _stream_of(n_ctx_tiles)
    in_specs, args = [], []
    for of, ob, z, zblk in o_groups:
        wg = of.shape[1]
        in_specs += [pl.BlockSpec((ROW_TILE, wg), lambda i: (i, 0)),
                     pl.BlockSpec((ROW_TILE, wg), lambda i: (i, 0)),
                     pl.BlockSpec((ROW_TILE, wg), lambda i, zblk=zblk: (i, zblk))]
        args += [of, ob, z]
    kdim = w.shape[0]
    in_specs += [pl.BlockSpec(gains.shape, lambda i: (0, 0)),
                 pl.BlockSpec((kdim, d), lambda i: (0, 0)),
                 pl.BlockSpec((ROW_TILE, d), lambda i: (i, 0)),
                 pl.BlockSpec((1, 8, d), lambda i: (stream(i), 0, 0)),
                 pl.BlockSpec((1, d), lambda i: (0, 0)),
                 pl.BlockSpec((d, LANES), lambda i: (0, 0))]
    args += [gains, w, x, mod, g2, router]
    row = lambda n: pl.BlockSpec((ROW_TILE, n), lambda i: (i, 0))
    return pl.pallas_call(
        functools.partial(_outproj_body, groups=len(o_groups), head_w=head_w),
        out_shape=(jax.ShapeDtypeStruct((nt, d), F32), jax.ShapeDtypeStruct((nt, d), BF16),
                   jax.ShapeDtypeStruct((nt, LANES), F32)),
        grid=(nt // ROW_TILE,),
        in_specs=in_specs,
        out_specs=(row(d), row(d), row(LANES)),
        compiler_params=_cparams(("parallel",)),
        name="out_proj",
    )(*args)


def _ffn_rows(cap):
    pack = 2 * SUBLANES
    return max(r for r in range(pack, FFN_ROWS + 1, pack) if cap % r == 0)


def _ffn_body(xs_ref, gate_ref, tok_ref, wg_ref, wu_ref, wd_ref, o_ref):
    f = pl.program_id(1)
    wg = wg_ref[0, 0].astype(BF16)
    wu = wu_ref[0, 0].astype(BF16)
    wd = wd_ref[0, 0].astype(BF16)
    cap = xs_ref.shape[1]
    rows = _ffn_rows(cap)

    @pl.when(f == 0)
    def _():
        o_ref[...] = jnp.zeros_like(o_ref)

    hids = []
    for r0 in range(0, cap, rows):
        xs = xs_ref[0, r0:r0 + rows, :]
        a = jnp.dot(xs, wg, preferred_element_type=F32)
        b = jnp.dot(xs, wu, preferred_element_type=F32)
        hids.append((_silu(a) * b).astype(BF16))
    d = xs_ref.shape[2]
    for r0, hid in zip(range(0, cap, rows), hids):
        o_ref[0, r0:r0 + rows, :d] += jnp.dot(hid, wd, preferred_element_type=F32)

    @pl.when(f == pl.num_programs(1) - 1)
    def _():
        o_ref[0, :, :d] = o_ref[0, :, :d] * gate_ref[0]
        o_ref[0, :, d:] = jnp.broadcast_to(tok_ref[0], (cap, LANES))


def _expert_ffn(xs, gate, tok, w_gate, w_up, w_down, layer):
    e, cap, d = xs.shape
    dff = w_gate.shape[3]
    return pl.pallas_call(
        _ffn_body,
        out_shape=jax.ShapeDtypeStruct((e, cap, d + LANES), F32),
        grid=(e, dff // FFN_TILE),
        in_specs=[pl.BlockSpec((1, cap, d), lambda i, f: (i, 0, 0)),
                  pl.BlockSpec((1, cap, 1), lambda i, f: (i, 0, 0)),
                  pl.BlockSpec((1, cap, 1), lambda i, f: (i, 0, 0)),
                  pl.BlockSpec((1, 1, d, FFN_TILE), lambda i, f: (layer, i, 0, f)),
                  pl.BlockSpec((1, 1, d, FFN_TILE), lambda i, f: (layer, i, 0, f)),
                  pl.BlockSpec((1, 1, FFN_TILE, d), lambda i, f: (layer, i, f, 0))],
        out_specs=pl.BlockSpec((1, cap, d + LANES), lambda i, f: (i, 0, 0)),
        compiler_params=_cparams(("parallel", "arbitrary")),
        name="expert_ffn",
    )(xs, gate, tok, w_gate, w_up, w_down)


def _combine_body(lo_ref, ys_hbm, x_ref, mod_ref, o_ref, win_ref, xwin_ref, acc_ref, sem, xsem):
    i = pl.program_id(0)
    n_steps = pl.num_programs(0)
    n_exp, cap, width = ys_hbm.shape
    d = width - LANES
    win = COMBINE_WIN
    slot = i % 2
    t0 = i * ROW_TILE

    def first_row(e, tile):
        return pl.multiple_of(jnp.minimum((lo_ref[e, tile] // SUBLANES) * SUBLANES, cap - win), SUBLANES)

    def window_copy(e, tile, buf):
        return pltpu.make_async_copy(ys_hbm.at[e, pl.ds(first_row(e, tile), win), :],
                                     win_ref.at[buf, e], sem.at[buf, e])

    @pl.when(i == 0)
    def _():
        for e in range(n_exp):
            window_copy(e, 0, 0).start()

    @pl.when(i + 1 < n_steps)
    def _():
        for e in range(n_exp):
            window_copy(e, i + 1, 1 - slot).start()

    row_id = lax.broadcasted_iota(jnp.int32, (win, LANES), 0)

    def valid_tok(tok, row0, lo, hi):
        slot_id = row_id + row0
        return jnp.where(jnp.logical_and(slot_id >= lo, slot_id < hi), tok, -1.0)

    def scattered(tok, rows):
        lane_tok = (lax.broadcasted_iota(jnp.int32, tok.shape, 1) + t0).astype(F32)
        onehot = jnp.concatenate([jnp.where(tok == lane_tok + float(j * LANES), 1.0, 0.0)
                                  for j in range(ROW_TILE // LANES)], axis=1).astype(BF16)
        rows_hi, rows_lo = _split2(rows)
        dot_t = lambda a, b: lax.dot_general(a, b, (((0,), (0,)), ((), ())), preferred_element_type=F32)
        return dot_t(onehot, rows_hi) + dot_t(onehot, rows_lo)

    for e in range(n_exp):
        window_copy(e, i, slot).wait()
    toks = jnp.concatenate([valid_tok(win_ref[slot, e, :, d:], first_row(e, i), lo_ref[e, i], lo_ref[e, i + 1])
                            for e in range(n_exp)], axis=0)
    acc_ref[...] = scattered(toks, win_ref[slot, :, :, :d].reshape(n_exp * win, d))

    def per_expert(e, carry):
        lo = lo_ref[e, i]
        hi = lo_ref[e, i + 1]
        row0 = first_row(e, i)

        def extra(w, c):
            want = row0 + w * win
            start = pl.multiple_of(jnp.minimum(want, cap - win), SUBLANES)
            cp = pltpu.make_async_copy(ys_hbm.at[e, pl.ds(start, win), :], xwin_ref, xsem)
            cp.start()
            cp.wait()
            tok = valid_tok(xwin_ref[:, d:], start, jnp.maximum(lo, want), hi)
            acc_ref[...] += scattered(tok, xwin_ref[:, :d])
            return c

        lax.fori_loop(1, (hi - row0 + win - 1) // win, extra, 0)
        return carry

    lax.fori_loop(0, n_exp, per_expert, 0)
    o_ref[...] = x_ref[...] + mod_ref[0][5:6] * acc_ref[...]


def _combine(lo, ysx, x, mod, n_ctx_tiles):
    nt, d = x.shape
    n_exp = ysx.shape[0]
    stream = _stream_of(n_ctx_tiles)
    grid_spec = pltpu.PrefetchScalarGridSpec(
        num_scalar_prefetch=1,
        grid=(nt // ROW_TILE,),
        in_specs=[pl.BlockSpec(memory_space=pl.ANY),
                  pl.BlockSpec((ROW_TILE, d), lambda i, lo: (i, 0)),
                  pl.BlockSpec((1, 8, d), lambda i, lo: (stream(i), 0, 0))],
        out_specs=pl.BlockSpec((ROW_TILE, d), lambda i, lo: (i, 0)),
        scratch_shapes=[pltpu.VMEM((2, n_exp, COMBINE_WIN, d + LANES), F32),
                        pltpu.VMEM((COMBINE_WIN, d + LANES), F32),
                        pltpu.VMEM((ROW_TILE, d), F32),
                        pltpu.SemaphoreType.DMA((2, n_exp)),
                        pltpu.SemaphoreType.DMA(())])
    return pl.pallas_call(
        _combine_body,
        out_shape=jax.ShapeDtypeStruct((nt, d), F32),
        grid_spec=grid_spec,
        compiler_params=_cparams(("arbitrary",)),
        name="moe_combine",
    )(lo, ysx, x, mod)


def _final_body(x_ref, g_ref, o_ref):
    x = x_ref[...]
    o_ref[...] = x * lax.rsqrt(jnp.mean(x * x, axis=-1, keepdims=True) + EPS) * g_ref[...]


def _final_norm(x, g, n_ctx_tiles):
    nt, d = x.shape
    n_tiles = nt // ROW_TILE - n_ctx_tiles
    return pl.pallas_call(
        _final_body,
        out_shape=jax.ShapeDtypeStruct((n_tiles * ROW_TILE, d), F32),
        grid=(n_tiles,),
        in_specs=[pl.BlockSpec((ROW_TILE, d), lambda i: (i + n_ctx_tiles, 0)),
                  pl.BlockSpec((1, d), lambda i: (0, 0))],
        out_specs=pl.BlockSpec((ROW_TILE, d), lambda i: (i, 0)),
        compiler_params=_cparams(("parallel",)),
        name="final_norm",
    )(x, g)


def _route(logits, cap):
    aff = jax.nn.softmax(logits, axis=-1)
    gate, idx = lax.top_k(aff.T, cap)
    idx, gate = lax.sort((idx, gate), dimension=1, num_keys=1)
    return gate, idx


def _moe(x_mid, h2, logits, mod, n_ctx_rows, w_gate, w_up, w_down, layer):
    nt, d = h2.shape
    n_lat = nt - n_ctx_rows
    lg = logits[:, :N_EXPERTS]
    gate_c, idx_c = _route(lg[:n_ctx_rows], EC_CAPACITY * n_ctx_rows // N_EXPERTS)
    gate_l, idx_l = _route(lg[n_ctx_rows:], EC_CAPACITY * n_lat // N_EXPERTS)
    gate = jnp.concatenate([gate_c, gate_l], axis=1)
    idx = jnp.concatenate([idx_c, idx_l + n_ctx_rows], axis=1)
    xs = h2[idx]
    ysx = _expert_ffn(xs, gate[..., None], idx.astype(F32)[..., None], w_gate, w_up, w_down, layer)
    bounds = jnp.arange(nt // ROW_TILE + 1, dtype=jnp.int32) * ROW_TILE
    lo = jnp.sum(idx[:, :, None] < bounds[None, None, :], axis=1, dtype=jnp.int32)
    return _combine(lo, ysx, x_mid, mod, n_ctx_rows // ROW_TILE)


def _rope_tables(n_ctx_rows, n_lat):
    quarter = DK_C // 4
    inv = ROPE_BASE ** (-jnp.arange(quarter, dtype=F32) / quarter)
    t = jnp.arange(n_lat)
    ang_r = (t // GRID_W).astype(F32)[:, None] * inv[None, :]
    ang_c = (t % GRID_W).astype(F32)[:, None] * inv[None, :]
    cos = jnp.concatenate([jnp.cos(ang_r)] * 2 + [jnp.cos(ang_c)] * 2, axis=1)
    sin = jnp.concatenate([-jnp.sin(ang_r), jnp.sin(ang_r), -jnp.sin(ang_c), jnp.sin(ang_c)], axis=1)
    cos = jnp.concatenate([jnp.ones((n_ctx_rows, DK_C), F32), cos], axis=0)
    sin = jnp.concatenate([jnp.zeros((n_ctx_rows, DK_C), F32), sin], axis=0)
    return cos, sin


def _pad_cols(a, n):
    return jnp.pad(a, ((0, 0), (0, n - a.shape[1])))


def kernel(x, c, ctx, c_ctx, ada_w, ada_b, norm_g, final_g, ev_w_in, ev_conv, ev_a_log, ev_dt_bias, ev_gdn_norm, ev_hgrn_lb, ev_hgrn_norm, ev_w_out, od_w_in, od_w_out, moe_router, moe_w_gate, moe_w_up, moe_w_down):
    depth = ada_w.shape[0]
    n_lat, d = x.shape[1], x.shape[2]
    n_ctx_rows = ctx.shape[1]
    assert x.shape[0] == 1 and n_ctx_rows % RET_T == 0 and n_lat % RET_T == 0 and RET_T == ROW_TILE
    n_ctx_tiles = n_ctx_rows // ROW_TILE

    xs = jnp.concatenate([ctx[0], x[0]], axis=0)
    cond = jnp.zeros((8, d), F32).at[0].set(jax.nn.silu(c_ctx)).at[1].set(jax.nn.silu(c[0]))
    p_l = jax.nn.softmax(ev_hgrn_lb.astype(F32), axis=1)
    lbs = jnp.cumsum(p_l, axis=1) - p_l[:, :1]
    cos, sin = _rope_tables(n_ctx_rows, n_lat)
    router = jnp.pad(moe_router, ((0, 0), (0, 0), (0, LANES - N_EXPERTS)))

    wka, wva, wkb, wvb = H_A * DK_A, H_A * DV_A, H_B * DK_B, H_B * DV_B
    n_small = 4 * H_A
    big = 2 * wka + 2 * wva + 3 * wkb + 2 * wvb
    small_at = 2 * wka + 2 * wva

    for l in range(depth):
        last = l == depth - 1
        i = l // 2
        mod = _dense(cond, ada_w[l].astype(BF16), ada_b[l][None, :], 6 * d // 4)
        mod = jnp.pad(mod[:2].reshape(2, 6, d), ((0, 0), (0, 2), (0, 0)))
        g1 = norm_g[l, 0][None, :]
        g2 = norm_g[l, 1][None, :]
        if l % 2 == 0:
            w_in = ev_w_in[i]
            qb_at = small_at + n_small
            fb_at = qb_at + wkb
            ib_at = fb_at + 2 * wkb
            w_perm = jnp.concatenate([w_in[:, :small_at], w_in[:, fb_at:ib_at], w_in[:, qb_at:fb_at],
                                      w_in[:, ib_at:],
                                      _pad_cols(w_in[:, small_at:small_at + n_small], LANES)], axis=1)
            p = _proj(xs, g1, mod, w_perm.astype(BF16), n_ctx_tiles)
            feat = _even_feat(p, jnp.pad(ev_conv[i], ((0, 8 - CONV_K), (0, 0))), n_ctx_tiles)
            par = jnp.zeros((8, LANES), F32)
            par = par.at[0, :2 * H_A].set(-jnp.exp(ev_a_log[i].astype(F32)).reshape(-1))
            par = par.at[1, :2 * H_A].set(ev_dt_bias[i].astype(F32).reshape(-1))
            oa_f, oa_b = _gdn_scan(feat, p, par, big // LANES, n_ctx_rows // SCAN_T)
            lb = jnp.pad(lbs[:, i], ((0, 6), (0, 0)))
            ob_f, ob_b = _gla_scan(p, lb, 6, 2, 7, n_ctx_rows // SCAN_T)
            gains = jnp.zeros((8, LANES), F32).at[0].set(ev_gdn_norm[i]).at[1].set(ev_hgrn_norm[i])
            groups = [(oa_f, oa_b, p, 3), (ob_f, ob_b, p, 8)]
            xs, h2, logits = _outproj(groups, gains, ev_w_out[i].astype(BF16), xs, mod, g2,
                                      router[l], DV_A, n_ctx_tiles)
        else:
            p = _proj(xs, g1, mod, od_w_in[i].astype(BF16), n_ctx_tiles)
            o_f, o_b = _ret_scan(p, cos, sin, n_ctx_rows // RET_T)
            gains = jnp.ones((8, DV_C), F32)
            groups = [(o_f, o_b, p, 2)]
            xs, h2, logits = _outproj(groups, gains, od_w_out[i].astype(BF16), xs, mod, g2,
                                      router[l], DV_C, n_ctx_tiles)
        xs = _moe(xs, h2, logits, mod, n_ctx_rows, moe_w_gate, moe_w_up, moe_w_down, l)
    return _final_norm(xs, final_g[None, :], n_ctx_tiles)[None]
```

```python
import functools
import math

import numpy as np
import jax
import jax.numpy as jnp
from jax import lax
from jax.experimental import pallas as pl
from jax.experimental.pallas import tpu as pltpu

F32 = jnp.float32
BF16 = jnp.bfloat16

GRID_W = 64
EPS = 1e-6
F_TINY = 1e-30
H_A, DK_A, DV_A = 4, 128, 128
CONV_K = 5
H_B, DK_B, DV_B = 4, 128, 128
H_C, DK_C, DV_C = 4, 256, 512
ROPE_BASE = 10000.0
N_EXPERTS = 16
EC_CAPACITY = 2

LANES = 128
SUBLANES = 8
ROW_TILE = 256
SCAN_T = 64
SCAN_GROUP = 4
SUB_T = 16
RET_T = 256
FFN_TILE = 256
FFN_ROWS = 512
COMBINE_WIN = 64
VMEM_LIMIT = 56 * 1024 * 1024


def _cparams(sem):
    return pltpu.CompilerParams(dimension_semantics=sem, vmem_limit_bytes=VMEM_LIMIT)


def _mm(a, b):
    return jnp.dot(a.astype(BF16), b.astype(BF16), preferred_element_type=F32)


def _mm_nt(a, b):
    return lax.dot_general(a.astype(BF16), b.astype(BF16), (((1,), (1,)), ((), ())),
                           preferred_element_type=F32)


def _mm_tn(a, b):
    return lax.dot_general(a.astype(BF16), b.astype(BF16), (((0,), (0,)), ((), ())),
                           preferred_element_type=F32)


def _split2(x):
    hi = x.astype(BF16)
    return hi, (x - hi.astype(F32)).astype(BF16)


def _mm3(a, b):
    ah, al = _split2(a)
    bh, bl = _split2(b)
    dot = lambda x, y: jnp.dot(x, y, preferred_element_type=F32)
    return dot(ah, bh) + (dot(ah, bl) + dot(al, bh))


def _cumdot(m, g):
    w = g.shape[1]
    hi = g.astype(BF16)
    r1 = g - hi.astype(F32)
    mid = r1.astype(BF16)
    lo = (r1 - mid.astype(F32)).astype(BF16)
    out = jnp.dot(m, jnp.concatenate([hi, mid, lo], axis=1), preferred_element_type=F32)
    return out[:, :w] + (out[:, w:2 * w] + out[:, 2 * w:])


def _sigmoid(x):
    return 1.0 / (1.0 + jnp.exp(-x))


def _silu(x):
    return x * _sigmoid(x)


def _softplus(x):
    return jnp.maximum(x, 0.0) + jnp.log(1.0 + jnp.exp(-jnp.abs(x)))


def _modulated(x, g, shift, scale):
    ms = jnp.mean(x * x, axis=-1, keepdims=True)
    return x * lax.rsqrt(ms + EPS) * g * (1.0 + scale) + shift


def _tri(n, reverse, strict):
    t = lax.broadcasted_iota(jnp.int32, (n, n), 0)
    s = lax.broadcasted_iota(jnp.int32, (n, n), 1)
    if reverse:
        return (s > t) if strict else (s >= t)
    return (s < t) if strict else (s <= t)


def _stream_of(n_ctx_tiles):
    return lambda i: jnp.where(i >= n_ctx_tiles, 1, 0)


def _bwd_chunk(n_ctx, n_all):
    return lambda i: jnp.where(i < n_ctx, n_ctx - 1 - i, n_all + n_ctx - 1 - i)


def _dense_body(a_ref, w_ref, b_ref, o_ref):
    o_ref[...] = _mm(a_ref[...], w_ref[...]) + b_ref[...]


def _dense(a, w, b, tn):
    m, k = a.shape
    n = w.shape[1]
    return pl.pallas_call(
        _dense_body,
        out_shape=jax.ShapeDtypeStruct((m, n), F32),
        grid=(n // tn,),
        in_specs=[pl.BlockSpec((m, k), lambda j: (0, 0)),
                  pl.BlockSpec((k, tn), lambda j: (0, j)),
                  pl.BlockSpec((1, tn), lambda j: (0, j))],
        out_specs=pl.BlockSpec((m, tn), lambda j: (0, j)),
        compiler_params=_cparams(("arbitrary",)),
        name="ada_dense",
    )(a, w, b)


def _proj_body(x_ref, g_ref, mod_ref, w_ref, o_ref):
    m = mod_ref[0]
    h = _modulated(x_ref[...], g_ref[...], m[0:1], m[1:2])
    o_ref[...] = jnp.dot(h.astype(BF16), w_ref[...], preferred_element_type=F32)


def _proj(x, g, mod, w, n_ctx_tiles):
    nt, d = x.shape
    n = w.shape[1]
    stream = _stream_of(n_ctx_tiles)
    return pl.pallas_call(
        _proj_body,
        out_shape=jax.ShapeDtypeStruct((nt, n), F32),
        grid=(nt // ROW_TILE,),
        in_specs=[pl.BlockSpec((ROW_TILE, d), lambda i: (i, 0)),
                  pl.BlockSpec((1, d), lambda i: (0, 0)),
                  pl.BlockSpec((1, 8, d), lambda i: (stream(i), 0, 0)),
                  pl.BlockSpec((d, n), lambda i: (0, 0))],
        out_specs=pl.BlockSpec((ROW_TILE, n), lambda i: (i, 0)),
        compiler_params=_cparams(("parallel",)),
        name="in_proj",
    )(x, g, mod, w)


def _feat_body(cur_ref, prev_ref, next_ref, w_ref, o_ref, ext_ref, *, n_ctx_tiles, n_tiles):
    i = pl.program_id(0)
    halo = SUBLANES
    prev_ok = jnp.logical_and(i != 0, i != n_ctx_tiles)
    next_ok = jnp.logical_and(i != n_ctx_tiles - 1, i != n_tiles - 1)
    ext_ref[0:halo, :] = jnp.where(prev_ok, prev_ref[...], 0.0)
    ext_ref[halo:halo + ROW_TILE, :] = cur_ref[...]
    ext_ref[halo + ROW_TILE:, :] = jnp.where(next_ok, next_ref[...], 0.0)
    acc = None
    for j in range(CONV_K):
        start = halo - CONV_K // 2 + j
        term = ext_ref[start:start + ROW_TILE, :] * w_ref[j:j + 1, :]
        acc = term if acc is None else acc + term
    y = _silu(acc)
    wq = H_A * DK_A
    for h in range(2 * H_A):
        lo = h * DK_A
        blk = y[:, lo:lo + DK_A]
        nrm = blk * lax.rsqrt(jnp.sum(blk * blk, axis=-1, keepdims=True) + EPS)
        if lo < wq:
            nrm = nrm * (DK_A ** -0.5)
        o_ref[:, lo:lo + DK_A] = nrm
    o_ref[:, 2 * wq:] = y[:, 2 * wq:]


def _even_feat(p, conv_w, n_ctx_tiles):
    nt = p.shape[0]
    wc = conv_w.shape[1]
    n_tiles = nt // ROW_TILE
    per = ROW_TILE // SUBLANES
    last_blk = nt // SUBLANES - 1
    return pl.pallas_call(
        functools.partial(_feat_body, n_ctx_tiles=n_ctx_tiles, n_tiles=n_tiles),
        out_shape=jax.ShapeDtypeStruct((nt, wc), F32),
        grid=(n_tiles,),
        in_specs=[pl.BlockSpec((ROW_TILE, wc), lambda i: (i, 0)),
                  pl.BlockSpec((SUBLANES, wc), lambda i: (jnp.maximum(i * per - 1, 0), 0)),
                  pl.BlockSpec((SUBLANES, wc), lambda i: (jnp.minimum((i + 1) * per, last_blk), 0)),
                  pl.BlockSpec((8, wc), lambda i: (0, 0))],
        out_specs=pl.BlockSpec((ROW_TILE, wc), lambda i: (i, 0)),
        scratch_shapes=[pltpu.VMEM((ROW_TILE + 2 * SUBLANES, wc), F32)],
        compiler_params=_cparams(("parallel",)),
        name="even_feat",
    )(p, p, p, conv_w)


def _unit_tri_inverses(mats, n):
    r = lax.broadcasted_iota(jnp.int32, (n, n), 0)
    c = lax.broadcasted_iota(jnp.int32, (n, n), 1)
    eye = jnp.where(r == c, 1.0, 0.0)
    ps = [eye - a for a in mats]
    aks = [_mm3(a, a) for a in mats]
    for _ in range(int(math.log2(n)) - 2):
        outs = [_mm3(jnp.concatenate([ak, p], axis=0), ak) for ak, p in zip(aks, ps)]
        aks = [o[:n] for o in outs]
        ps = [p + o[n:] for p, o in zip(ps, outs)]
    return [p + _mm3(p, ak) for p, ak in zip(ps, aks)]


def _gdn_body(qf, kf, vf, smf, qb, kb, vb, smb, par_ref, of_ref, ob_ref, s_ref):
    @pl.when(pl.program_id(0) == 0)
    def _():
        s_ref[...] = jnp.zeros_like(s_ref)

    t_len = SCAN_T
    refs = ((qf, kf, vf, smf, of_ref), (qb, kb, vb, smb, ob_ref))
    incl = [_tri(t_len, d == 1, False) for d in range(2)]
    strict = [_tri(t_len, d == 1, True) for d in range(2)]
    st = []
    for d in range(2):
        cum = jnp.where(incl[d], 1.0, 0.0).astype(BF16)
        for u in range(SCAN_GROUP):
            r0 = u * t_len
            sm = refs[d][3][r0:r0 + t_len, :]
            g_all = par_ref[0:1, :] * _softplus(sm + par_ref[1:2, :])
            beta_all = _sigmoid(sm)
            gc = _cumdot(cum, g_all)
            gc_t = gc.T
            for h in range(H_A):
                c = d * H_A + h
                lo = h * DK_A
                gcol = gc[:, c:c + 1]
                dec = jnp.where(incl[d], jnp.exp(jnp.where(incl[d], gcol - gc_t[c:c + 1, :], 0.0)), 0.0)
                beta = beta_all[:, 2 * H_A + c:2 * H_A + c + 1]
                q = refs[d][0][r0:r0 + t_len, lo:lo + DK_A]
                k = refs[d][1][r0:r0 + t_len, lo:lo + DK_A]
                v = refs[d][2][r0:r0 + t_len, lo:lo + DK_A]
                last_row = 0 if d == 1 else t_len - 1
                glast = gc[last_row:last_row + 1, c:c + 1]
                st.append(dict(c=c, lo=lo, d=d, u=u, r0=r0, dec=dec, beta=beta, q=q, k=k, v=v, kb=k * beta,
                               eg=jnp.exp(gcol), kt=k * jnp.exp(glast - gcol), cd=jnp.exp(glast)))
    kks = [_mm_nt(x["kb"], x["k"]) for x in st]
    qks = [_mm_nt(x["q"], x["k"]) for x in st]
    invs = _unit_tri_inverses([jnp.where(strict[x["d"]], kk * x["dec"], 0.0) for x, kk in zip(st, kks)], t_len)
    uws = [_mm3(inv, jnp.concatenate([x["v"] * x["beta"], x["kb"] * x["eg"]], axis=1)) for x, inv in zip(st, invs)]
    for x, qk, uw in zip(st, qks, uws):
        x["attn"] = qk * x["dec"]
        x["uw"] = uw
    state = {c: s_ref[c] for c in range(2 * H_A)}
    for step in range(SCAN_GROUP):
        now = [x for x in st if x["u"] == (SCAN_GROUP - 1 - step if x["d"] == 1 else step)]
        ss = [state[x["c"]] for x in now]
        vns = [x["uw"][:, :DV_A] - _mm(x["uw"][:, DV_A:], s) for x, s in zip(now, ss)]
        inters = [_mm(x["q"] * x["eg"], s) for x, s in zip(now, ss)]
        intras = [_mm(x["attn"], vn) for x, vn in zip(now, vns)]
        upds = [_mm_tn(x["kt"], vn) for x, vn in zip(now, vns)]
        for x, s, inter, intra, upd in zip(now, ss, inters, intras, upds):
            refs[x["d"]][4][x["r0"]:x["r0"] + t_len, x["lo"]:x["lo"] + DV_A] = inter + intra
            state[x["c"]] = x["cd"] * s + upd
    for c in range(2 * H_A):
        s_ref[c] = state[c]


def _gdn_scan(feat, p, par, small_blk, n_ctx):
    nt = feat.shape[0]
    rows = SCAN_T * SCAN_GROUP
    n_all = nt // rows
    wk = H_A * DK_A
    bwd = _bwd_chunk(n_ctx, n_all)

    def specs(chunk):
        return [pl.BlockSpec((rows, wk), lambda i: (chunk(i), 0)),
                pl.BlockSpec((rows, wk), lambda i: (chunk(i), 1)),
                pl.BlockSpec((rows, wk), lambda i: (chunk(i), 2)),
                pl.BlockSpec((rows, LANES), lambda i: (chunk(i), small_blk))]

    fwd = lambda i: i
    out = jax.ShapeDtypeStruct((nt, H_A * DV_A), F32)
    return pl.pallas_call(
        _gdn_body,
        out_shape=(out, out),
        grid=(n_all,),
        in_specs=specs(fwd) + specs(bwd) + [pl.BlockSpec((8, LANES), lambda i: (0, 0))],
        out_specs=(pl.BlockSpec((rows, H_A * DV_A), lambda i: (i, 0)),
                   pl.BlockSpec((rows, H_A * DV_A), lambda i: (bwd(i), 0))),
        scratch_shapes=[pltpu.VMEM((2 * H_A, DK_A, DV_A), F32)],
        compiler_params=_cparams(("arbitrary",)),
        name="gdn_scan",
    )(feat, feat, feat, p, feat, feat, feat, p, par)


def _gla_diag_terms(qi, g2i, reverse):
    n = SUB_T
    s3 = lax.broadcasted_iota(jnp.int32, (n, n, LANES), 0)
    t3 = lax.broadcasted_iota(jnp.int32, (n, n, LANES), 1)
    ok = (s3 >= t3) if reverse else (s3 <= t3)
    x = jnp.where(ok, jnp.exp2(g2i[None, :, :] - g2i[:, None, :]) * qi[None, :, :], 0.0)
    return x.reshape(n * n, LANES).astype(BF16)


def _gla_body(qf, ff, vf, qb, fb, vb, lb_ref, of_ref, ob_ref, s_ref):
    @pl.when(pl.program_id(0) == 0)
    def _():
        s_ref[...] = jnp.zeros_like(s_ref)

    t_len, n = SCAN_T, SUB_T
    nb = t_len // n
    refs = ((qf, ff, vf, of_ref), (qb, fb, vb, ob_ref))
    cum = [jnp.where(_tri(t_len, d == 1, False), 1.0, 0.0).astype(BF16) for d in range(2)]
    row = lax.broadcasted_iota(jnp.int32, (n * n, n), 0)
    lane = lax.broadcasted_iota(jnp.int32, (n * n, n), 1)
    keep = (row // n) == lane
    sel = (lax.broadcasted_iota(jnp.int32, (n, n * n), 1) % n
           == lax.broadcasted_iota(jnp.int32, (n, n * n), 0))
    sel = jnp.where(sel, 1.0, 0.0).astype(BF16)
    st = []
    for d in range(2):
        for u in range(SCAN_GROUP):
            c0 = u * t_len
            for h in range(H_B):
                lo = h * DK_B
                fo = d * H_B * DK_B + lo
                fx = refs[d][1][c0:c0 + t_len, fo:fo + DK_B]
                lb = lb_ref[d:d + 1, lo:lo + DK_B]
                f = lb + (1.0 - lb) * _sigmoid(fx)
                st.append(dict(d=d, u=u, c0=c0, lo=lo, c=d * H_B + h, log_f=jnp.log(jnp.maximum(f, F_TINY)),
                               kk=(1.0 - lb) * _sigmoid(-fx),
                               q=_silu(refs[d][0][c0:c0 + t_len, lo:lo + DK_B]) * (DK_B ** -0.5),
                               v=refs[d][2][c0:c0 + t_len, lo:lo + DV_B], intra=[]))
    for x in st:
        x["gc"] = _cumdot(cum[x["d"]], x["log_f"])
        last_row = 0 if x["d"] == 1 else t_len - 1
        x["glast"] = x["gc"][last_row:last_row + 1, :]
        x["g2"] = x["gc"] * math.log2(math.e)
    for bi in range(nb):
        r0 = bi * n
        terms = [_gla_diag_terms(x["q"][r0:r0 + n], x["g2"][r0:r0 + n], x["d"] == 1) for x in st]
        sums = [_mm_nt(t, x["kk"][r0:r0 + n]) for t, x in zip(terms, st)]
        diags = [jnp.dot(sel, jnp.where(keep, r, 0.0).astype(BF16), preferred_element_type=F32) for r in sums]
        offs = []
        for x in st:
            reverse = x["d"] == 1
            if reverse and bi < nb - 1:
                p0, p1 = r0 + n, t_len
                ref = x["gc"][p0:p0 + 1, :]
            elif (not reverse) and bi > 0:
                p0, p1 = 0, r0
                ref = x["gc"][r0 - 1:r0, :]
            else:
                offs.append(None)
                continue
            qt = x["q"][r0:r0 + n] * jnp.exp(x["gc"][r0:r0 + n] - ref)
            kt = x["kk"][p0:p1] * jnp.exp(ref - x["gc"][p0:p1])
            offs.append((_mm_nt(qt, kt), p0, p1))
        for x, diag, off in zip(st, diags, offs):
            o_blk = _mm(diag, x["v"][r0:r0 + n])
            if off is not None:
                o_blk = o_blk + _mm(off[0], x["v"][off[1]:off[2]])
            x["intra"].append(o_blk)
    for x in st:
        x["qe"] = x["q"] * jnp.exp(x["gc"])
        x["k_tail"] = x["kk"] * jnp.exp(x["glast"] - x["gc"])
    state = {c: s_ref[c] for c in range(2 * H_B)}
    for step in range(SCAN_GROUP):
        now = [x for x in st if x["u"] == (SCAN_GROUP - 1 - step if x["d"] == 1 else step)]
        ss = [state[x["c"]] for x in now]
        inters = [_mm_nt(x["qe"], s_t) for x, s_t in zip(now, ss)]
        upds = [_mm_tn(x["v"], x["k_tail"]) for x in now]
        for x, s_t, inter, upd in zip(now, ss, inters, upds):
            refs[x["d"]][3][x["c0"]:x["c0"] + t_len, x["lo"]:x["lo"] + DV_B] = (
                inter + jnp.concatenate(x["intra"], axis=0))
            state[x["c"]] = jnp.exp(x["glast"]) * s_t + upd
    for c in range(2 * H_B):
        s_ref[c] = state[c]


def _gla_scan(p, lb, q_blk, f_blk, v_blk, n_ctx):
    nt = p.shape[0]
    rows = SCAN_T * SCAN_GROUP
    n_all = nt // rows
    wk = H_B * DK_B
    bwd = _bwd_chunk(n_ctx, n_all)

    def specs(chunk):
        return [pl.BlockSpec((rows, wk), lambda i: (chunk(i), q_blk)),
                pl.BlockSpec((rows, 2 * wk), lambda i: (chunk(i), f_blk)),
                pl.BlockSpec((rows, wk), lambda i: (chunk(i), v_blk))]

    fwd = lambda i: i
    out = jax.ShapeDtypeStruct((nt, H_B * DV_B), F32)
    return pl.pallas_call(
        _gla_body,
        out_shape=(out, out),
        grid=(n_all,),
        in_specs=specs(fwd) + specs(bwd) + [pl.BlockSpec((8, wk), lambda i: (0, 0))],
        out_specs=(pl.BlockSpec((rows, H_B * DV_B), lambda i: (i, 0)),
                   pl.BlockSpec((rows, H_B * DV_B), lambda i: (bwd(i), 0))),
        scratch_shapes=[pltpu.VMEM((2 * H_B, DV_B, DK_B), F32)],
        compiler_params=_cparams(("arbitrary",)),
        name="gla_scan",
    )(p, p, p, p, p, p, lb)


def _rope(x, cos, sin):
    parts = []
    for j in range(x.shape[1] // LANES):
        blk = x[:, j * LANES:(j + 1) * LANES]
        parts.append(blk * cos[:, j * LANES:(j + 1) * LANES]
                     + pltpu.roll(blk, LANES // 2, 1) * sin[:, j * LANES:(j + 1) * LANES])
    return jnp.concatenate(parts, axis=1)


def _ret_log_decay(d, h):
    e = (H_C - 1 - h) if d == 1 else h
    return math.log(1.0 - 2.0 ** (-5.0 - e))


def _chunk_rotary(rt_ref, ct_ref):
    rt = rt_ref[0]
    ct = ct_ref[0]
    reps = RET_T // GRID_W
    by_row = jnp.concatenate([jnp.broadcast_to(rt[j:j + 1, :], (GRID_W, 2 * LANES)) for j in range(reps)], axis=0)
    by_col = jnp.concatenate([ct] * reps, axis=0)
    cos = jnp.concatenate([by_row[:, :LANES], by_col[:, :LANES]], axis=1)
    sin = jnp.concatenate([by_row[:, LANES:], by_col[:, LANES:]], axis=1)
    return cos, sin


def _ret_dir(q_ref, k_ref, v_ref, rt_ref, ct_ref, dec_ref, qs_ref, ks_ref, o_ref, s_ref, d):
    cos, sin = _chunk_rotary(rt_ref, ct_ref)
    for h in range(H_C):
        c = d * H_C + h
        chunk_decay = math.exp(_ret_log_decay(d, h) * RET_T)
        q = _rope(q_ref[:, h * DK_C:(h + 1) * DK_C], cos, sin)
        k = _rope(k_ref[:, h * DK_C:(h + 1) * DK_C] * (DK_C ** -0.5), cos, sin)
        v = v_ref[:, h * DV_C:(h + 1) * DV_C]
        attn = _mm_nt(q, k) * dec_ref[c]
        qs = jnp.concatenate([qs_ref[c]] * (DK_C // LANES), axis=1)
        ks = jnp.concatenate([ks_ref[c]] * (DK_C // LANES), axis=1)
        s = s_ref[c]
        o_ref[:, h * DV_C:(h + 1) * DV_C] = _mm(q * qs, s) + _mm(attn, v)
        s_ref[c] = chunk_decay * s + _mm_tn(k * ks, v)


def _ret_body(qf, kf, vf, cf, sf, qb, kb, vb, cb, sb, dec_ref, qs_ref, ks_ref, of_ref, ob_ref, s_ref):
    @pl.when(pl.program_id(0) == 0)
    def _():
        s_ref[...] = jnp.zeros_like(s_ref)

    _ret_dir(qf, kf, vf, cf, sf, dec_ref, qs_ref, ks_ref, of_ref, s_ref, 0)
    _ret_dir(qb, kb, vb, cb, sb, dec_ref, qs_ref, ks_ref, ob_ref, s_ref, 1)


def _ret_tables():
    t = np.arange(RET_T, dtype=np.float64)
    dec, qs, ks = [], [], []
    for d in range(2):
        for h in range(H_C):
            lg = _ret_log_decay(d, h)
            pos = (RET_T - 1 - t) if d == 1 else t
            diff = pos[:, None] - pos[None, :]
            dec.append(np.where(diff >= 0, np.exp(lg * np.maximum(diff, 0.0)), 0.0))
            qs.append(np.broadcast_to(np.exp(lg * (pos + 1.0))[:, None], (RET_T, LANES)))
            ks.append(np.broadcast_to(np.exp(lg * (RET_T - 1.0 - pos))[:, None], (RET_T, LANES)))
    f = lambda xs: jnp.asarray(np.stack(xs), F32)
    return f(dec), f(qs), f(ks)


def _ret_scan(p, row_tab, col_tab, n_ctx):
    nt = p.shape[0]
    n_all = nt // RET_T
    wk, wv = H_C * DK_C, H_C * DV_C
    bwd = _bwd_chunk(n_ctx, n_all)
    stream = _stream_of(n_ctx)
    dec, qs, ks = _ret_tables()

    def specs(chunk):
        return [pl.BlockSpec((RET_T, wk), lambda i: (chunk(i), 0)),
                pl.BlockSpec((RET_T, wk), lambda i: (chunk(i), 1)),
                pl.BlockSpec((RET_T, wv), lambda i: (chunk(i), 1)),
                pl.BlockSpec((1, 8, 2 * LANES), lambda i: (chunk(i), 0, 0)),
                pl.BlockSpec((1, GRID_W, 2 * LANES), lambda i: (stream(chunk(i)), 0, 0))]

    full = lambda a: pl.BlockSpec(a.shape, lambda i: (0,) * a.ndim)
    fwd = lambda i: i
    out = jax.ShapeDtypeStruct((nt, wv), F32)
    return pl.pallas_call(
        _ret_body,
        out_shape=(out, out),
        grid=(n_all,),
        in_specs=specs(fwd) + specs(bwd) + [full(dec), full(qs), full(ks)],
        out_specs=(pl.BlockSpec((RET_T, wv), lambda i: (i, 0)),
                   pl.BlockSpec((RET_T, wv), lambda i: (bwd(i), 0))),
        scratch_shapes=[pltpu.VMEM((2 * H_C, DK_C, DV_C), F32)],
        compiler_params=_cparams(("arbitrary",)),
        name="ret_scan",
    )(p, p, p, row_tab, col_tab, p, p, p, row_tab, col_tab, dec, qs, ks)


def _outproj_body(*refs, groups, head_w):
    n_in = 3 * groups
    o_refs = refs[:n_in]
    gain_ref, w_ref, x_ref, mod_ref, g2_ref, r_ref = refs[n_in:n_in + 6]
    xo_ref, h_ref, lg_ref = refs[n_in + 6:]
    ys = []
    for gi in range(groups):
        of_ref, ob_ref, z_ref = o_refs[3 * gi:3 * gi + 3]
        wgrp = of_ref.shape[1]
        for h in range(wgrp // head_w):
            lo = h * head_w
            o = of_ref[:, lo:lo + head_w] + ob_ref[:, lo:lo + head_w]
            rms = o * lax.rsqrt(jnp.mean(o * o, axis=-1, keepdims=True) + EPS)
            gain = gain_ref[gi:gi + 1, 0:head_w]
            ys.append((rms * gain * _silu(z_ref[:, lo:lo + head_w])).astype(BF16))
    y = jnp.concatenate(ys, axis=1)
    m = mod_ref[0]
    x_new = x_ref[...] + m[2:3] * jnp.dot(y, w_ref[...], preferred_element_type=F32)
    xo_ref[...] = x_new
    h2 = _modulated(x_new, g2_ref[...], m[3:4], m[4:5])
    h_ref[...] = h2.astype(BF16)
    lg_ref[...] = _mm3(h2, r_ref[...])


def _outproj(o_groups, gains, w, x, mod, g2, router, head_w, n_ctx_tiles):
    nt, d = x.shape
    stream = _stream_of(n_ctx_tiles)
    in_specs, args = [], []
    for of, ob, z, zblk in o_groups:
        wg = of.shape[1]
        in_specs += [pl.BlockSpec((ROW_TILE, wg), lambda i: (i, 0)),
                     pl.BlockSpec((ROW_TILE, wg), lambda i: (i, 0)),
                     pl.BlockSpec((ROW_TILE, wg), lambda i, zblk=zblk: (i, zblk))]
        args += [of, ob, z]
    kdim = w.shape[0]
    in_specs += [pl.BlockSpec(gains.shape, lambda i: (0, 0)),
                 pl.BlockSpec((kdim, d), lambda i: (0, 0)),
                 pl.BlockSpec((ROW_TILE, d), lambda i: (i, 0)),
                 pl.BlockSpec((1, 8, d), lambda i: (stream(i), 0, 0)),
                 pl.BlockSpec((1, d), lambda i: (0, 0)),
                 pl.BlockSpec((d, LANES), lambda i: (0, 0))]
    args += [gains, w, x, mod, g2, router]
    row = lambda n: pl.BlockSpec((ROW_TILE, n), lambda i: (i, 0))
    return pl.pallas_call(
        functools.partial(_outproj_body, groups=len(o_groups), head_w=head_w),
        out_shape=(jax.ShapeDtypeStruct((nt, d), F32), jax.ShapeDtypeStruct((nt, d), BF16),
                   jax.ShapeDtypeStruct((nt, LANES), F32)),
        grid=(nt // ROW_TILE,),
        in_specs=in_specs,
        out_specs=(row(d), row(d), row(LANES)),
        compiler_params=_cparams(("parallel",)),
        name="out_proj",
    )(*args)


def _ffn_rows(cap):
    pack = 2 * SUBLANES
    return max(r for r in range(pack, FFN_ROWS + 1, pack) if cap % r == 0)


def _ffn_body(xs_ref, gate_ref, tok_ref, wg_ref, wu_ref, wd_ref, o_ref):
    f = pl.program_id(1)
    wg = wg_ref[0, 0].astype(BF16)
    wu = wu_ref[0, 0].astype(BF16)
    wd = wd_ref[0, 0].astype(BF16)
    cap = xs_ref.shape[1]
    rows = _ffn_rows(cap)

    @pl.when(f == 0)
    def _():
        o_ref[...] = jnp.zeros_like(o_ref)

    hids = []
    for r0 in range(0, cap, rows):
        xs = xs_ref[0, r0:r0 + rows, :]
        a = jnp.dot(xs, wg, preferred_element_type=F32)
        b = jnp.dot(xs, wu, preferred_element_type=F32)
        hids.append((_silu(a) * b).astype(BF16))
    d = xs_ref.shape[2]
    for r0, hid in zip(range(0, cap, rows), hids):
        o_ref[0, r0:r0 + rows, :d] += jnp.dot(hid, wd, preferred_element_type=F32)

    @pl.when(f == pl.num_programs(1) - 1)
    def _():
        o_ref[0, :, :d] = o_ref[0, :, :d] * gate_ref[0]
        o_ref[0, :, d:] = jnp.broadcast_to(tok_ref[0], (cap, LANES))


def _expert_ffn(xs, gate, tok, w_gate, w_up, w_down, layer):
    e, cap, d = xs.shape
    dff = w_gate.shape[3]
    return pl.pallas_call(
        _ffn_body,
        out_shape=jax.ShapeDtypeStruct((e, cap, d + LANES), F32),
        grid=(e, dff // FFN_TILE),
        in_specs=[pl.BlockSpec((1, cap, d), lambda i, f: (i, 0, 0)),
                  pl.BlockSpec((1, cap, 1), lambda i, f: (i, 0, 0)),
                  pl.BlockSpec((1, cap, 1), lambda i, f: (i, 0, 0)),
                  pl.BlockSpec((1, 1, d, FFN_TILE), lambda i, f: (layer, i, 0, f)),
                  pl.BlockSpec((1, 1, d, FFN_TILE), lambda i, f: (layer, i, 0, f)),
                  pl.BlockSpec((1, 1, FFN_TILE, d), lambda i, f: (layer, i, f, 0))],
        out_specs=pl.BlockSpec((1, cap, d + LANES), lambda i, f: (i, 0, 0)),
        compiler_params=_cparams(("parallel", "arbitrary")),
        name="expert_ffn",
    )(xs, gate, tok, w_gate, w_up, w_down)


def _combine_body(lo_ref, ys_hbm, x_ref, mod_ref, o_ref, win_ref, xwin_ref, acc_ref, sem, xsem):
    i = pl.program_id(0)
    n_steps = pl.num_programs(0)
    n_exp, cap, width = ys_hbm.shape
    d = width - LANES
    win = COMBINE_WIN
    slot = i % 2
    t0 = i * ROW_TILE

    def first_row(e, tile):
        return pl.multiple_of(jnp.minimum((lo_ref[e, tile] // SUBLANES) * SUBLANES, cap - win), SUBLANES)

    def window_copy(e, tile, buf):
        return pltpu.make_async_copy(ys_hbm.at[e, pl.ds(first_row(e, tile), win), :],
                                     win_ref.at[buf, e], sem.at[buf, e])

    @pl.when(i == 0)
    def _():
        for e in range(n_exp):
            window_copy(e, 0, 0).start()

    @pl.when(i + 1 < n_steps)
    def _():
        for e in range(n_exp):
            window_copy(e, i + 1, 1 - slot).start()

    row_id = lax.broadcasted_iota(jnp.int32, (win, LANES), 0)

    def valid_tok(tok, row0, lo, hi):
        slot_id = row_id + row0
        return jnp.where(jnp.logical_and(slot_id >= lo, slot_id < hi), tok, -1.0)

    def scattered(tok, rows):
        lane_tok = (lax.broadcasted_iota(jnp.int32, tok.shape, 1) + t0).astype(F32)
        onehot = jnp.concatenate([jnp.where(tok == lane_tok + float(j * LANES), 1.0, 0.0)
                                  for j in range(ROW_TILE // LANES)], axis=1).astype(BF16)
        rows_hi, rows_lo = _split2(rows)
        dot_t = lambda a, b: lax.dot_general(a, b, (((0,), (0,)), ((), ())), preferred_element_type=F32)
        return dot_t(onehot, rows_hi) + dot_t(onehot, rows_lo)

    for e in range(n_exp):
        window_copy(e, i, slot).wait()
    toks = jnp.concatenate([valid_tok(win_ref[slot, e, :, d:], first_row(e, i), lo_ref[e, i], lo_ref[e, i + 1])
                            for e in range(n_exp)], axis=0)
    acc_ref[...] = scattered(toks, win_ref[slot, :, :, :d].reshape(n_exp * win, d))

    def per_expert(e, carry):
        lo = lo_ref[e, i]
        hi = lo_ref[e, i + 1]
        row0 = first_row(e, i)

        def extra(w, c):
            want = row0 + w * win
            start = pl.multiple_of(jnp.minimum(want, cap - win), SUBLANES)
            cp = pltpu.make_async_copy(ys_hbm.at[e, pl.ds(start, win), :], xwin_ref, xsem)
            cp.start()
            cp.wait()
            tok = valid_tok(xwin_ref[:, d:], start, jnp.maximum(lo, want), hi)
            acc_ref[...] += scattered(tok, xwin_ref[:, :d])
            return c

        lax.fori_loop(1, (hi - row0 + win - 1) // win, extra, 0)
        return carry

    lax.fori_loop(0, n_exp, per_expert, 0)
    o_ref[...] = x_ref[...] + mod_ref[0][5:6] * acc_ref[...]


def _combine(lo, ysx, x, mod, n_ctx_tiles):
    nt, d = x.shape
    n_exp = ysx.shape[0]
    stream = _stream_of(n_ctx_tiles)
    grid_spec = pltpu.PrefetchScalarGridSpec(
        num_scalar_prefetch=1,
        grid=(nt // ROW_TILE,),
        in_specs=[pl.BlockSpec(memory_space=pl.ANY),
                  pl.BlockSpec((ROW_TILE, d), lambda i, lo: (i, 0)),
                  pl.BlockSpec((1, 8, d), lambda i, lo: (stream(i), 0, 0))],
        out_specs=pl.BlockSpec((ROW_TILE, d), lambda i, lo: (i, 0)),
        scratch_shapes=[pltpu.VMEM((2, n_exp, COMBINE_WIN, d + LANES), F32),
                        pltpu.VMEM((COMBINE_WIN, d + LANES), F32),
                        pltpu.VMEM((ROW_TILE, d), F32),
                        pltpu.SemaphoreType.DMA((2, n_exp)),
                        pltpu.SemaphoreType.DMA(())])
    return pl.pallas_call(
        _combine_body,
        out_shape=jax.ShapeDtypeStruct((nt, d), F32),
        grid_spec=grid_spec,
        compiler_params=_cparams(("arbitrary",)),
        name="moe_combine",
    )(lo, ysx, x, mod)


def _final_body(x_ref, g_ref, o_ref):
    x = x_ref[...]
    o_ref[...] = x * lax.rsqrt(jnp.mean(x * x, axis=-1, keepdims=True) + EPS) * g_ref[...]


def _final_norm(x, g, n_ctx_tiles):
    nt, d = x.shape
    n_tiles = nt // ROW_TILE - n_ctx_tiles
    return pl.pallas_call(
        _final_body,
        out_shape=jax.ShapeDtypeStruct((n_tiles * ROW_TILE, d), F32),
        grid=(n_tiles,),
        in_specs=[pl.BlockSpec((ROW_TILE, d), lambda i: (i + n_ctx_tiles, 0)),
                  pl.BlockSpec((1, d), lambda i: (0, 0))],
        out_specs=pl.BlockSpec((ROW_TILE, d), lambda i: (i, 0)),
        compiler_params=_cparams(("parallel",)),
        name="final_norm",
    )(x, g)


def _route(logits, cap):
    aff = jax.nn.softmax(logits, axis=-1)
    gate, idx = lax.top_k(aff.T, cap)
    idx, gate = lax.sort((idx, gate), dimension=1, num_keys=1)
    return gate, idx


def _moe(x_mid, h2, logits, mod, n_ctx_rows, w_gate, w_up, w_down, layer):
    nt, d = h2.shape
    n_lat = nt - n_ctx_rows
    lg = logits[:, :N_EXPERTS]
    gate_c, idx_c = _route(lg[:n_ctx_rows], EC_CAPACITY * n_ctx_rows // N_EXPERTS)
    gate_l, idx_l = _route(lg[n_ctx_rows:], EC_CAPACITY * n_lat // N_EXPERTS)
    gate = jnp.concatenate([gate_c, gate_l], axis=1)
    idx = jnp.concatenate([idx_c, idx_l + n_ctx_rows], axis=1)
    xs = h2[idx]
    ysx = _expert_ffn(xs, gate[..., None], idx.astype(F32)[..., None], w_gate, w_up, w_down, layer)
    bounds = jnp.arange(nt // ROW_TILE + 1, dtype=jnp.int32) * ROW_TILE
    lo = jnp.sum(idx[:, :, None] < bounds[None, None, :], axis=1, dtype=jnp.int32)
    return _combine(lo, ysx, x_mid, mod, n_ctx_rows // ROW_TILE)


def _rope_tables(n_ctx_rows, n_lat):
    quarter = DK_C // 4
    inv = ROPE_BASE ** (-jnp.arange(quarter, dtype=F32) / quarter)

    def entries(pos):
        ang = pos.astype(F32)[:, None] * inv[None, :]
        return jnp.concatenate([jnp.cos(ang), jnp.cos(ang), -jnp.sin(ang), jnp.sin(ang)], axis=1)

    ident = jnp.concatenate([jnp.ones((1, LANES), F32), jnp.zeros((1, LANES), F32)], axis=1)
    per = RET_T // GRID_W
    rows = entries(jnp.arange(n_lat // GRID_W)).reshape(n_lat // RET_T, per, 2 * LANES)
    rows = jnp.concatenate([jnp.broadcast_to(ident, (n_ctx_rows // RET_T, per, 2 * LANES)), rows], axis=0)
    row_tab = jnp.pad(rows, ((0, 0), (0, 8 - per), (0, 0)))
    col_tab = jnp.stack([jnp.broadcast_to(ident, (GRID_W, 2 * LANES)), entries(jnp.arange(GRID_W))])
    return row_tab, col_tab


def _pad_cols(a, n):
    return jnp.pad(a, ((0, 0), (0, n - a.shape[1])))


def kernel(x, c, ctx, c_ctx, ada_w, ada_b, norm_g, final_g, ev_w_in, ev_conv, ev_a_log, ev_dt_bias, ev_gdn_norm, ev_hgrn_lb, ev_hgrn_norm, ev_w_out, od_w_in, od_w_out, moe_router, moe_w_gate, moe_w_up, moe_w_down):
    depth = ada_w.shape[0]
    n_lat, d = x.shape[1], x.shape[2]
    n_ctx_rows = ctx.shape[1]
    assert x.shape[0] == 1 and n_ctx_rows % RET_T == 0 and n_lat % RET_T == 0 and RET_T == ROW_TILE
    n_ctx_tiles = n_ctx_rows // ROW_TILE

    xs = jnp.concatenate([ctx[0], x[0]], axis=0)
    cond = jnp.zeros((8, d), F32).at[0].set(jax.nn.silu(c_ctx)).at[1].set(jax.nn.silu(c[0]))
    p_l = jax.nn.softmax(ev_hgrn_lb.astype(F32), axis=1)
    lbs = jnp.cumsum(p_l, axis=1) - p_l[:, :1]
    row_tab, col_tab = _rope_tables(n_ctx_rows, n_lat)
    router = jnp.pad(moe_router, ((0, 0), (0, 0), (0, LANES - N_EXPERTS)))

    wka, wva, wkb, wvb = H_A * DK_A, H_A * DV_A, H_B * DK_B, H_B * DV_B
    n_small = 4 * H_A
    big = 2 * wka + 2 * wva + 3 * wkb + 2 * wvb
    small_at = 2 * wka + 2 * wva

    for l in range(depth):
        last = l == depth - 1
        i = l // 2
        mod = _dense(cond, ada_w[l].astype(BF16), ada_b[l][None, :], 6 * d // 4)
        mod = jnp.pad(mod[:2].reshape(2, 6, d), ((0, 0), (0, 2), (0, 0)))
        g1 = norm_g[l, 0][None, :]
        g2 = norm_g[l, 1][None, :]
        if l % 2 == 0:
            w_in = ev_w_in[i]
            qb_at = small_at + n_small
            fb_at = qb_at + wkb
            ib_at = fb_at + 2 * wkb
            w_perm = jnp.concatenate([w_in[:, :small_at], w_in[:, fb_at:ib_at], w_in[:, qb_at:fb_at],
                                      w_in[:, ib_at:],
                                      _pad_cols(w_in[:, small_at:small_at + n_small], LANES)], axis=1)
            p = _proj(xs, g1, mod, w_perm.astype(BF16), n_ctx_tiles)
            feat = _even_feat(p, jnp.pad(ev_conv[i], ((0, 8 - CONV_K), (0, 0))), n_ctx_tiles)
            par = jnp.zeros((8, LANES), F32)
            par = par.at[0, :2 * H_A].set(-jnp.exp(ev_a_log[i].astype(F32)).reshape(-1))
            par = par.at[1, :2 * H_A].set(ev_dt_bias[i].astype(F32).reshape(-1))
            n_ctx_blocks = n_ctx_rows // (SCAN_T * SCAN_GROUP)
            oa_f, oa_b = _gdn_scan(feat, p, par, big // LANES, n_ctx_blocks)
            lb = jnp.pad(lbs[:, i], ((0, 6), (0, 0)))
            ob_f, ob_b = _gla_scan(p, lb, 6, 2, 7, n_ctx_blocks)
            gains = jnp.zeros((8, LANES), F32).at[0].set(ev_gdn_norm[i]).at[1].set(ev_hgrn_norm[i])
            groups = [(oa_f, oa_b, p, 3), (ob_f, ob_b, p, 8)]
            xs, h2, logits = _outproj(groups, gains, ev_w_out[i].astype(BF16), xs, mod, g2,
                                      router[l], DV_A, n_ctx_tiles)
        else:
            p = _proj(xs, g1, mod, od_w_in[i].astype(BF16), n_ctx_tiles)
            o_f, o_b = _ret_scan(p, row_tab, col_tab, n_ctx_rows // RET_T)
            gains = jnp.ones((8, DV_C), F32)
            groups = [(o_f, o_b, p, 2)]
            xs, h2, logits = _outproj(groups, gains, od_w_out[i].astype(BF16), xs, mod, g2,
                                      router[l], DV_C, n_ctx_tiles)
        xs = _moe(xs, h2, logits, mod, n_ctx_rows, moe_w_gate, moe_w_up, moe_w_down, l)
    return _final_norm(xs, final_g[None, :], n_ctx_tiles)[None]
```

```python
import functools
import math

import numpy as np
import jax
import jax.numpy as jnp
from jax import lax
from jax.experimental import pallas as pl
from jax.experimental.pallas import tpu as pltpu

F32 = jnp.float32
BF16 = jnp.bfloat16

GRID_W = 64
EPS = 1e-6
F_TINY = 1e-30
H_A, DK_A, DV_A = 4, 128, 128
CONV_K = 5
H_B, DK_B, DV_B = 4, 128, 128
H_C, DK_C, DV_C = 4, 256, 512
ROPE_BASE = 10000.0
N_EXPERTS = 16
EC_CAPACITY = 2

LANES = 128
SUBLANES = 8
ROW_TILE = 256
SCAN_T = 64
SCAN_GROUP = 4
SUB_T = 16
RET_T = 256
FFN_TILE = 256
FFN_ROWS = 512
COMBINE_WIN = 64
VMEM_LIMIT = 56 * 1024 * 1024


def _cparams(sem):
    return pltpu.CompilerParams(dimension_semantics=sem, vmem_limit_bytes=VMEM_LIMIT)


def _mm(a, b):
    return jnp.dot(a.astype(BF16), b.astype(BF16), preferred_element_type=F32)


def _mm_nt(a, b):
    return lax.dot_general(a.astype(BF16), b.astype(BF16), (((1,), (1,)), ((), ())),
                           preferred_element_type=F32)


def _mm_tn(a, b):
    return lax.dot_general(a.astype(BF16), b.astype(BF16), (((0,), (0,)), ((), ())),
                           preferred_element_type=F32)


def _split2(x):
    hi = x.astype(BF16)
    return hi, (x - hi.astype(F32)).astype(BF16)


def _mm3(a, b):
    ah, al = _split2(a)
    bh, bl = _split2(b)
    dot = lambda x, y: jnp.dot(x, y, preferred_element_type=F32)
    return dot(ah, bh) + (dot(ah, bl) + dot(al, bh))


def _cumdot(m, g):
    w = g.shape[1]
    hi = g.astype(BF16)
    r1 = g - hi.astype(F32)
    mid = r1.astype(BF16)
    lo = (r1 - mid.astype(F32)).astype(BF16)
    out = jnp.dot(m, jnp.concatenate([hi, mid, lo], axis=1), preferred_element_type=F32)
    return out[:, :w] + (out[:, w:2 * w] + out[:, 2 * w:])


def _sigmoid(x):
    return 1.0 / (1.0 + jnp.exp(-x))


def _silu(x):
    return x * _sigmoid(x)


def _softplus(x):
    return jnp.maximum(x, 0.0) + jnp.log(1.0 + jnp.exp(-jnp.abs(x)))


def _modulated(x, g, shift, scale):
    ms = jnp.mean(x * x, axis=-1, keepdims=True)
    return x * lax.rsqrt(ms + EPS) * g * (1.0 + scale) + shift


def _tri(n, reverse, strict):
    t = lax.broadcasted_iota(jnp.int32, (n, n), 0)
    s = lax.broadcasted_iota(jnp.int32, (n, n), 1)
    if reverse:
        return (s > t) if strict else (s >= t)
    return (s < t) if strict else (s <= t)


def _stream_of(n_ctx_tiles):
    return lambda i: jnp.where(i >= n_ctx_tiles, 1, 0)


def _bwd_chunk(n_ctx, n_all):
    return lambda i: jnp.where(i < n_ctx, n_ctx - 1 - i, n_all + n_ctx - 1 - i)


def _dense_body(a_ref, w_ref, b_ref, o_ref):
    o_ref[...] = _mm(a_ref[...], w_ref[...]) + b_ref[...]


def _dense(a, w, b, tn):
    m, k = a.shape
    n = w.shape[1]
    return pl.pallas_call(
        _dense_body,
        out_shape=jax.ShapeDtypeStruct((m, n), F32),
        grid=(n // tn,),
        in_specs=[pl.BlockSpec((m, k), lambda j: (0, 0)),
                  pl.BlockSpec((k, tn), lambda j: (0, j)),
                  pl.BlockSpec((1, tn), lambda j: (0, j))],
        out_specs=pl.BlockSpec((m, tn), lambda j: (0, j)),
        compiler_params=_cparams(("arbitrary",)),
        name="ada_dense",
    )(a, w, b)


def _proj_body(x_ref, g_ref, mod_ref, w_ref, o_ref):
    m = mod_ref[0]
    h = _modulated(x_ref[...], g_ref[...], m[0:1], m[1:2])
    o_ref[...] = jnp.dot(h.astype(BF16), w_ref[...], preferred_element_type=F32)


def _proj(x, g, mod, w, n_ctx_tiles):
    nt, d = x.shape
    n = w.shape[1]
    stream = _stream_of(n_ctx_tiles)
    return pl.pallas_call(
        _proj_body,
        out_shape=jax.ShapeDtypeStruct((nt, n), F32),
        grid=(nt // ROW_TILE,),
        in_specs=[pl.BlockSpec((ROW_TILE, d), lambda i: (i, 0)),
                  pl.BlockSpec((1, d), lambda i: (0, 0)),
                  pl.BlockSpec((1, 8, d), lambda i: (stream(i), 0, 0)),
                  pl.BlockSpec((d, n), lambda i: (0, 0))],
        out_specs=pl.BlockSpec((ROW_TILE, n), lambda i: (i, 0)),
        compiler_params=_cparams(("parallel",)),
        name="in_proj",
    )(x, g, mod, w)


def _feat_body(cur_ref, prev_ref, next_ref, w_ref, o_ref, ext_ref, *, n_ctx_tiles, n_tiles):
    i = pl.program_id(0)
    halo = SUBLANES
    prev_ok = jnp.logical_and(i != 0, i != n_ctx_tiles)
    next_ok = jnp.logical_and(i != n_ctx_tiles - 1, i != n_tiles - 1)
    ext_ref[0:halo, :] = jnp.where(prev_ok, prev_ref[...], 0.0)
    ext_ref[halo:halo + ROW_TILE, :] = cur_ref[...]
    ext_ref[halo + ROW_TILE:, :] = jnp.where(next_ok, next_ref[...], 0.0)
    acc = None
    for j in range(CONV_K):
        start = halo - CONV_K // 2 + j
        term = ext_ref[start:start + ROW_TILE, :] * w_ref[j:j + 1, :]
        acc = term if acc is None else acc + term
    y = _silu(acc)
    wq = H_A * DK_A
    for h in range(2 * H_A):
        lo = h * DK_A
        blk = y[:, lo:lo + DK_A]
        nrm = blk * lax.rsqrt(jnp.sum(blk * blk, axis=-1, keepdims=True) + EPS)
        if lo < wq:
            nrm = nrm * (DK_A ** -0.5)
        o_ref[:, lo:lo + DK_A] = nrm
    o_ref[:, 2 * wq:] = y[:, 2 * wq:]


def _even_feat(p, conv_w, n_ctx_tiles):
    nt = p.shape[0]
    wc = conv_w.shape[1]
    n_tiles = nt // ROW_TILE
    per = ROW_TILE // SUBLANES
    last_blk = nt // SUBLANES - 1
    return pl.pallas_call(
        functools.partial(_feat_body, n_ctx_tiles=n_ctx_tiles, n_tiles=n_tiles),
        out_shape=jax.ShapeDtypeStruct((nt, wc), F32),
        grid=(n_tiles,),
        in_specs=[pl.BlockSpec((ROW_TILE, wc), lambda i: (i, 0)),
                  pl.BlockSpec((SUBLANES, wc), lambda i: (jnp.maximum(i * per - 1, 0), 0)),
                  pl.BlockSpec((SUBLANES, wc), lambda i: (jnp.minimum((i + 1) * per, last_blk), 0)),
                  pl.BlockSpec((8, wc), lambda i: (0, 0))],
        out_specs=pl.BlockSpec((ROW_TILE, wc), lambda i: (i, 0)),
        scratch_shapes=[pltpu.VMEM((ROW_TILE + 2 * SUBLANES, wc), F32)],
        compiler_params=_cparams(("parallel",)),
        name="even_feat",
    )(p, p, p, conv_w)


def _unit_tri_inverses(mats, n):
    r = lax.broadcasted_iota(jnp.int32, (n, n), 0)
    c = lax.broadcasted_iota(jnp.int32, (n, n), 1)
    eye = jnp.where(r == c, 1.0, 0.0)
    ps = [eye - a for a in mats]
    aks = [_mm3(a, a) for a in mats]
    for _ in range(int(math.log2(n)) - 2):
        outs = [_mm3(jnp.concatenate([ak, p], axis=0), ak) for ak, p in zip(aks, ps)]
        aks = [o[:n] for o in outs]
        ps = [p + o[n:] for p, o in zip(ps, outs)]
    return [p + _mm3(p, ak) for p, ak in zip(ps, aks)]


def _gdn_body(qf, kf, vf, smf, qb, kb, vb, smb, par_ref, of_ref, ob_ref, s_ref):
    @pl.when(pl.program_id(0) == 0)
    def _():
        s_ref[...] = jnp.zeros_like(s_ref)

    t_len = SCAN_T
    refs = ((qf, kf, vf, smf, of_ref), (qb, kb, vb, smb, ob_ref))
    incl = [_tri(t_len, d == 1, False) for d in range(2)]
    strict = [_tri(t_len, d == 1, True) for d in range(2)]
    st = []
    for d in range(2):
        cum = jnp.where(incl[d], 1.0, 0.0).astype(BF16)
        for u in range(SCAN_GROUP):
            r0 = u * t_len
            sm = refs[d][3][r0:r0 + t_len, :]
            g_all = par_ref[0:1, :] * _softplus(sm + par_ref[1:2, :])
            beta_all = _sigmoid(sm)
            gc = _cumdot(cum, g_all)
            gc_t = gc.T
            for h in range(H_A):
                c = d * H_A + h
                lo = h * DK_A
                gcol = gc[:, c:c + 1]
                dec = jnp.where(incl[d], jnp.exp(jnp.where(incl[d], gcol - gc_t[c:c + 1, :], 0.0)), 0.0)
                beta = beta_all[:, 2 * H_A + c:2 * H_A + c + 1]
                q = refs[d][0][r0:r0 + t_len, lo:lo + DK_A]
                k = refs[d][1][r0:r0 + t_len, lo:lo + DK_A]
                v = refs[d][2][r0:r0 + t_len, lo:lo + DK_A]
                last_row = 0 if d == 1 else t_len - 1
                glast = gc[last_row:last_row + 1, c:c + 1]
                st.append(dict(c=c, lo=lo, d=d, u=u, r0=r0, dec=dec, beta=beta, q=q, k=k, v=v, kb=k * beta,
                               eg=jnp.exp(gcol), kt=k * jnp.exp(glast - gcol), cd=jnp.exp(glast)))
    kks = [_mm_nt(x["kb"], x["k"]) for x in st]
    qks = [_mm_nt(x["q"], x["k"]) for x in st]
    invs = _unit_tri_inverses([jnp.where(strict[x["d"]], kk * x["dec"], 0.0) for x, kk in zip(st, kks)], t_len)
    uws = [_mm3(inv, jnp.concatenate([x["v"] * x["beta"], x["kb"] * x["eg"]], axis=1)) for x, inv in zip(st, invs)]
    for x, qk, uw in zip(st, qks, uws):
        x["attn"] = qk * x["dec"]
        x["uw"] = uw
    state = {c: s_ref[c] for c in range(2 * H_A)}
    for step in range(SCAN_GROUP):
        now = [x for x in st if x["u"] == (SCAN_GROUP - 1 - step if x["d"] == 1 else step)]
        ss = [state[x["c"]] for x in now]
        vns = [x["uw"][:, :DV_A] - _mm(x["uw"][:, DV_A:], s) for x, s in zip(now, ss)]
        inters = [_mm(x["q"] * x["eg"], s) for x, s in zip(now, ss)]
        intras = [_mm(x["attn"], vn) for x, vn in zip(now, vns)]
        upds = [_mm_tn(x["kt"], vn) for x, vn in zip(now, vns)]
        for x, s, inter, intra, upd in zip(now, ss, inters, intras, upds):
            refs[x["d"]][4][x["r0"]:x["r0"] + t_len, x["lo"]:x["lo"] + DV_A] = inter + intra
            state[x["c"]] = x["cd"] * s + upd
    for c in range(2 * H_A):
        s_ref[c] = state[c]


def _gdn_scan(feat, p, par, small_blk, n_ctx):
    nt = feat.shape[0]
    rows = SCAN_T * SCAN_GROUP
    n_all = nt // rows
    wk = H_A * DK_A
    bwd = _bwd_chunk(n_ctx, n_all)

    def specs(chunk):
        return [pl.BlockSpec((rows, wk), lambda i: (chunk(i), 0)),
                pl.BlockSpec((rows, wk), lambda i: (chunk(i), 1)),
                pl.BlockSpec((rows, wk), lambda i: (chunk(i), 2)),
                pl.BlockSpec((rows, LANES), lambda i: (chunk(i), small_blk))]

    fwd = lambda i: i
    out = jax.ShapeDtypeStruct((nt, H_A * DV_A), F32)
    return pl.pallas_call(
        _gdn_body,
        out_shape=(out, out),
        grid=(n_all,),
        in_specs=specs(fwd) + specs(bwd) + [pl.BlockSpec((8, LANES), lambda i: (0, 0))],
        out_specs=(pl.BlockSpec((rows, H_A * DV_A), lambda i: (i, 0)),
                   pl.BlockSpec((rows, H_A * DV_A), lambda i: (bwd(i), 0))),
        scratch_shapes=[pltpu.VMEM((2 * H_A, DK_A, DV_A), F32)],
        compiler_params=_cparams(("arbitrary",)),
        name="gdn_scan",
    )(feat, feat, feat, p, feat, feat, feat, p, par)


def _gla_diag_terms(qi, g2i, reverse):
    n = SUB_T
    s3 = lax.broadcasted_iota(jnp.int32, (n, n, LANES), 0)
    t3 = lax.broadcasted_iota(jnp.int32, (n, n, LANES), 1)
    ok = (s3 >= t3) if reverse else (s3 <= t3)
    x = jnp.where(ok, jnp.exp2(g2i[None, :, :] - g2i[:, None, :]) * qi[None, :, :], 0.0)
    return x.reshape(n * n, LANES).astype(BF16)


def _gla_body(qf, ff, vf, qb, fb, vb, lb_ref, of_ref, ob_ref, s_ref):
    @pl.when(pl.program_id(0) == 0)
    def _():
        s_ref[...] = jnp.zeros_like(s_ref)

    t_len, n = SCAN_T, SUB_T
    nb = t_len // n
    refs = ((qf, ff, vf, of_ref), (qb, fb, vb, ob_ref))
    cum = [jnp.where(_tri(t_len, d == 1, False), 1.0, 0.0).astype(BF16) for d in range(2)]
    row = lax.broadcasted_iota(jnp.int32, (n * n, n), 0)
    lane = lax.broadcasted_iota(jnp.int32, (n * n, n), 1)
    keep = (row // n) == lane
    sel = (lax.broadcasted_iota(jnp.int32, (n, n * n), 1) % n
           == lax.broadcasted_iota(jnp.int32, (n, n * n), 0))
    sel = jnp.where(sel, 1.0, 0.0).astype(BF16)
    st = []
    for d in range(2):
        for u in range(SCAN_GROUP):
            c0 = u * t_len
            for h in range(H_B):
                lo = h * DK_B
                fo = d * H_B * DK_B + lo
                fx = refs[d][1][c0:c0 + t_len, fo:fo + DK_B]
                lb = lb_ref[d:d + 1, lo:lo + DK_B]
                f = lb + (1.0 - lb) * _sigmoid(fx)
                st.append(dict(d=d, u=u, c0=c0, lo=lo, c=d * H_B + h, log_f=jnp.log(jnp.maximum(f, F_TINY)),
                               kk=(1.0 - lb) * _sigmoid(-fx),
                               q=_silu(refs[d][0][c0:c0 + t_len, lo:lo + DK_B]) * (DK_B ** -0.5),
                               v=refs[d][2][c0:c0 + t_len, lo:lo + DV_B], intra=[]))
    for x in st:
        x["gc"] = _cumdot(cum[x["d"]], x["log_f"])
        last_row = 0 if x["d"] == 1 else t_len - 1
        x["glast"] = x["gc"][last_row:last_row + 1, :]
        x["g2"] = x["gc"] * math.log2(math.e)
    for bi in range(nb):
        r0 = bi * n
        terms = [_gla_diag_terms(x["q"][r0:r0 + n], x["g2"][r0:r0 + n], x["d"] == 1) for x in st]
        sums = [_mm_nt(t, x["kk"][r0:r0 + n]) for t, x in zip(terms, st)]
        diags = [jnp.dot(sel, jnp.where(keep, r, 0.0).astype(BF16), preferred_element_type=F32) for r in sums]
        offs = []
        for x in st:
            reverse = x["d"] == 1
            if reverse and bi < nb - 1:
                p0, p1 = r0 + n, t_len
                ref = x["gc"][p0:p0 + 1, :]
            elif (not reverse) and bi > 0:
                p0, p1 = 0, r0
                ref = x["gc"][r0 - 1:r0, :]
            else:
                offs.append(None)
                continue
            qt = x["q"][r0:r0 + n] * jnp.exp(x["gc"][r0:r0 + n] - ref)
            kt = x["kk"][p0:p1] * jnp.exp(ref - x["gc"][p0:p1])
            offs.append((_mm_nt(qt, kt), p0, p1))
        for x, diag, off in zip(st, diags, offs):
            o_blk = _mm(diag, x["v"][r0:r0 + n])
            if off is not None:
                o_blk = o_blk + _mm(off[0], x["v"][off[1]:off[2]])
            x["intra"].append(o_blk)
    for x in st:
        x["qe"] = x["q"] * jnp.exp(x["gc"])
        x["k_tail"] = x["kk"] * jnp.exp(x["glast"] - x["gc"])
    state = {c: s_ref[c] for c in range(2 * H_B)}
    for step in range(SCAN_GROUP):
        now = [x for x in st if x["u"] == (SCAN_GROUP - 1 - step if x["d"] == 1 else step)]
        ss = [state[x["c"]] for x in now]
        inters = [_mm_nt(x["qe"], s_t) for x, s_t in zip(now, ss)]
        upds = [_mm_tn(x["v"], x["k_tail"]) for x in now]
        for x, s_t, inter, upd in zip(now, ss, inters, upds):
            refs[x["d"]][3][x["c0"]:x["c0"] + t_len, x["lo"]:x["lo"] + DV_B] = (
                inter + jnp.concatenate(x["intra"], axis=0))
            state[x["c"]] = jnp.exp(x["glast"]) * s_t + upd
    for c in range(2 * H_B):
        s_ref[c] = state[c]


def _gla_scan(p, lb, q_blk, f_blk, v_blk, n_ctx):
    nt = p.shape[0]
    rows = SCAN_T * SCAN_GROUP
    n_all = nt // rows
    wk = H_B * DK_B
    bwd = _bwd_chunk(n_ctx, n_all)

    def specs(chunk):
        return [pl.BlockSpec((rows, wk), lambda i: (chunk(i), q_blk)),
                pl.BlockSpec((rows, 2 * wk), lambda i: (chunk(i), f_blk)),
                pl.BlockSpec((rows, wk), lambda i: (chunk(i), v_blk))]

    fwd = lambda i: i
    out = jax.ShapeDtypeStruct((nt, H_B * DV_B), F32)
    return pl.pallas_call(
        _gla_body,
        out_shape=(out, out),
        grid=(n_all,),
        in_specs=specs(fwd) + specs(bwd) + [pl.BlockSpec((8, wk), lambda i: (0, 0))],
        out_specs=(pl.BlockSpec((rows, H_B * DV_B), lambda i: (i, 0)),
                   pl.BlockSpec((rows, H_B * DV_B), lambda i: (bwd(i), 0))),
        scratch_shapes=[pltpu.VMEM((2 * H_B, DV_B, DK_B), F32)],
        compiler_params=_cparams(("arbitrary",)),
        name="gla_scan",
    )(p, p, p, p, p, p, lb)


def _rope(x, cos, sin):
    parts = []
    for j in range(x.shape[1] // LANES):
        blk = x[:, j * LANES:(j + 1) * LANES]
        parts.append(blk * cos[:, j * LANES:(j + 1) * LANES]
                     + pltpu.roll(blk, LANES // 2, 1) * sin[:, j * LANES:(j + 1) * LANES])
    return jnp.concatenate(parts, axis=1)


def _ret_log_decay(d, h):
    e = (H_C - 1 - h) if d == 1 else h
    return math.log(1.0 - 2.0 ** (-5.0 - e))


def _chunk_rotary(rt_ref, ct_ref):
    rt = rt_ref[0]
    ct = ct_ref[0]
    reps = RET_T // GRID_W
    by_row = jnp.concatenate([jnp.broadcast_to(rt[j:j + 1, :], (GRID_W, 2 * LANES)) for j in range(reps)], axis=0)
    by_col = jnp.concatenate([ct] * reps, axis=0)
    cos = jnp.concatenate([by_row[:, :LANES], by_col[:, :LANES]], axis=1)
    sin = jnp.concatenate([by_row[:, LANES:], by_col[:, LANES:]], axis=1)
    return cos, sin


def _ret_dir(q_ref, k_ref, v_ref, rt_ref, ct_ref, dec_ref, qs_ref, ks_ref, o_ref, s_ref, d):
    cos, sin = _chunk_rotary(rt_ref, ct_ref)
    for h in range(H_C):
        c = d * H_C + h
        chunk_decay = math.exp(_ret_log_decay(d, h) * RET_T)
        q = _rope(q_ref[:, h * DK_C:(h + 1) * DK_C], cos, sin)
        k = _rope(k_ref[:, h * DK_C:(h + 1) * DK_C] * (DK_C ** -0.5), cos, sin)
        v = v_ref[:, h * DV_C:(h + 1) * DV_C]
        attn = _mm_nt(q, k) * dec_ref[c]
        qs = jnp.concatenate([qs_ref[c]] * (DK_C // LANES), axis=1)
        ks = jnp.concatenate([ks_ref[c]] * (DK_C // LANES), axis=1)
        s = s_ref[c]
        o_ref[:, h * DV_C:(h + 1) * DV_C] = _mm(q * qs, s) + _mm(attn, v)
        s_ref[c] = chunk_decay * s + _mm_tn(k * ks, v)


def _ret_body(qf, kf, vf, cf, sf, qb, kb, vb, cb, sb, dec_ref, qs_ref, ks_ref, of_ref, ob_ref, s_ref):
    @pl.when(pl.program_id(0) == 0)
    def _():
        s_ref[...] = jnp.zeros_like(s_ref)

    _ret_dir(qf, kf, vf, cf, sf, dec_ref, qs_ref, ks_ref, of_ref, s_ref, 0)
    _ret_dir(qb, kb, vb, cb, sb, dec_ref, qs_ref, ks_ref, ob_ref, s_ref, 1)


def _ret_tables():
    t = np.arange(RET_T, dtype=np.float64)
    dec, qs, ks = [], [], []
    for d in range(2):
        for h in range(H_C):
            lg = _ret_log_decay(d, h)
            pos = (RET_T - 1 - t) if d == 1 else t
            diff = pos[:, None] - pos[None, :]
            dec.append(np.where(diff >= 0, np.exp(lg * np.maximum(diff, 0.0)), 0.0))
            qs.append(np.broadcast_to(np.exp(lg * (pos + 1.0))[:, None], (RET_T, LANES)))
            ks.append(np.broadcast_to(np.exp(lg * (RET_T - 1.0 - pos))[:, None], (RET_T, LANES)))
    f = lambda xs: jnp.asarray(np.stack(xs), F32)
    return f(dec), f(qs), f(ks)


def _ret_scan(p, row_tab, col_tab, n_ctx):
    nt = p.shape[0]
    n_all = nt // RET_T
    wk, wv = H_C * DK_C, H_C * DV_C
    bwd = _bwd_chunk(n_ctx, n_all)
    stream = _stream_of(n_ctx)
    dec, qs, ks = _ret_tables()

    def specs(chunk):
        return [pl.BlockSpec((RET_T, wk), lambda i: (chunk(i), 0)),
                pl.BlockSpec((RET_T, wk), lambda i: (chunk(i), 1)),
                pl.BlockSpec((RET_T, wv), lambda i: (chunk(i), 1)),
                pl.BlockSpec((1, 8, 2 * LANES), lambda i: (chunk(i), 0, 0)),
                pl.BlockSpec((1, GRID_W, 2 * LANES), lambda i: (stream(chunk(i)), 0, 0))]

    full = lambda a: pl.BlockSpec(a.shape, lambda i: (0,) * a.ndim)
    fwd = lambda i: i
    out = jax.ShapeDtypeStruct((nt, wv), F32)
    return pl.pallas_call(
        _ret_body,
        out_shape=(out, out),
        grid=(n_all,),
        in_specs=specs(fwd) + specs(bwd) + [full(dec), full(qs), full(ks)],
        out_specs=(pl.BlockSpec((RET_T, wv), lambda i: (i, 0)),
                   pl.BlockSpec((RET_T, wv), lambda i: (bwd(i), 0))),
        scratch_shapes=[pltpu.VMEM((2 * H_C, DK_C, DV_C), F32)],
        compiler_params=_cparams(("arbitrary",)),
        name="ret_scan",
    )(p, p, p, row_tab, col_tab, p, p, p, row_tab, col_tab, dec, qs, ks)


def _outproj_body(*refs, groups, head_w):
    n_in = 3 * groups
    o_refs = refs[:n_in]
    gain_ref, w_ref, x_ref, mod_ref, g2_ref, r_ref = refs[n_in:n_in + 6]
    xo_ref, h_ref, lg_ref = refs[n_in + 6:]
    ys = []
    for gi in range(groups):
        of_ref, ob_ref, z_ref = o_refs[3 * gi:3 * gi + 3]
        wgrp = of_ref.shape[1]
        for h in range(wgrp // head_w):
            lo = h * head_w
            o = of_ref[:, lo:lo + head_w] + ob_ref[:, lo:lo + head_w]
            rms = o * lax.rsqrt(jnp.mean(o * o, axis=-1, keepdims=True) + EPS)
            gain = gain_ref[gi:gi + 1, 0:head_w]
            ys.append((rms * gain * _silu(z_ref[:, lo:lo + head_w])).astype(BF16))
    y = jnp.concatenate(ys, axis=1)
    m = mod_ref[0]
    x_new = x_ref[...] + m[2:3] * jnp.dot(y, w_ref[...], preferred_element_type=F32)
    xo_ref[...] = x_new
    h2 = _modulated(x_new, g2_ref[...], m[3:4], m[4:5])
    h_ref[...] = h2
    lg_ref[...] = _mm3(h2, r_ref[...])


def _outproj(o_groups, gains, w, x, mod, g2, router, head_w, n_ctx_tiles):
    nt, d = x.shape
    stream = _stream_of(n_ctx_tiles)
    in_specs, args = [], []
    for of, ob, z, zblk in o_groups:
        wg = of.shape[1]
        in_specs += [pl.BlockSpec((ROW_TILE, wg), lambda i: (i, 0)),
                     pl.BlockSpec((ROW_TILE, wg), lambda i: (i, 0)),
                     pl.BlockSpec((ROW_TILE, wg), lambda i, zblk=zblk: (i, zblk))]
        args += [of, ob, z]
    kdim = w.shape[0]
    in_specs += [pl.BlockSpec(gains.shape, lambda i: (0, 0)),
                 pl.BlockSpec((kdim, d), lambda i: (0, 0)),
                 pl.BlockSpec((ROW_TILE, d), lambda i: (i, 0)),
                 pl.BlockSpec((1, 8, d), lambda i: (stream(i), 0, 0)),
                 pl.BlockSpec((1, d), lambda i: (0, 0)),
                 pl.BlockSpec((d, LANES), lambda i: (0, 0))]
    args += [gains, w, x, mod, g2, router]
    row = lambda n: pl.BlockSpec((ROW_TILE, n), lambda i: (i, 0))
    return pl.pallas_call(
        functools.partial(_outproj_body, groups=len(o_groups), head_w=head_w),
        out_shape=(jax.ShapeDtypeStruct((nt, d), F32), jax.ShapeDtypeStruct((nt, d), F32),
                   jax.ShapeDtypeStruct((nt, LANES), F32)),
        grid=(nt // ROW_TILE,),
        in_specs=in_specs,
        out_specs=(row(d), row(d), row(LANES)),
        compiler_params=_cparams(("parallel",)),
        name="out_proj",
    )(*args)


def _ffn_rows(cap):
    pack = 2 * SUBLANES
    return max(r for r in range(pack, FFN_ROWS + 1, pack) if cap % r == 0)


def _ffn_body(idx_ref, h_hbm, gate_ref, tok_ref, wg_ref, wu_ref, wd_ref, o_ref, xbuf, x_scr, sem):
    e = pl.program_id(0)
    f = pl.program_id(1)
    wg = wg_ref[0, 0].astype(BF16)
    wu = wu_ref[0, 0].astype(BF16)
    wd = wd_ref[0, 0].astype(BF16)
    cap, d = x_scr.shape
    rows = _ffn_rows(cap)

    def row_copy(expert, r):
        return pltpu.make_async_copy(h_hbm.at[pl.ds(idx_ref[expert, r], 1), :], xbuf.at[pl.ds(r, 1), :], sem)

    def start_rows(expert):
        def pair(j, carry):
            row_copy(expert, 2 * j).start(priority=0)
            row_copy(expert, 2 * j + 1).start(priority=1)
            return carry
        lax.fori_loop(0, cap // 2, pair, 0)

    @pl.when(jnp.logical_and(e == 0, f == 0))
    def _():
        start_rows(0)

    @pl.when(f == 0)
    def _():
        def wait_row(r, carry):
            row_copy(e, r).wait()
            return carry
        lax.fori_loop(0, cap, wait_row, 0, unroll=2 * SUBLANES)
        for r0 in range(0, cap, rows):
            x_scr[r0:r0 + rows, :] = xbuf[r0:r0 + rows, :].astype(BF16)
        o_ref[...] = jnp.zeros_like(o_ref)

    @pl.when(jnp.logical_and(f == 0, e + 1 < pl.num_programs(0)))
    def _():
        start_rows(e + 1)

    hids = []
    for r0 in range(0, cap, rows):
        xs = x_scr[r0:r0 + rows, :]
        a = jnp.dot(xs, wg, preferred_element_type=F32)
        b = jnp.dot(xs, wu, preferred_element_type=F32)
        hids.append((_silu(a) * b).astype(BF16))
    for r0, hid in zip(range(0, cap, rows), hids):
        o_ref[0, r0:r0 + rows, :d] += jnp.dot(hid, wd, preferred_element_type=F32)

    @pl.when(f == pl.num_programs(1) - 1)
    def _():
        o_ref[0, :, :d] = o_ref[0, :, :d] * gate_ref[0]
        o_ref[0, :, d:] = jnp.broadcast_to(tok_ref[0], (cap, LANES))


def _expert_ffn(idx, h, gate, tok, w_gate, w_up, w_down, layer):
    e, cap = idx.shape
    d = h.shape[1]
    dff = w_gate.shape[3]
    grid_spec = pltpu.PrefetchScalarGridSpec(
        num_scalar_prefetch=1,
        grid=(e, dff // FFN_TILE),
        in_specs=[pl.BlockSpec(memory_space=pl.ANY),
                  pl.BlockSpec((1, cap, 1), lambda i, f, ix: (i, 0, 0)),
                  pl.BlockSpec((1, cap, 1), lambda i, f, ix: (i, 0, 0)),
                  pl.BlockSpec((1, 1, d, FFN_TILE), lambda i, f, ix: (layer, i, 0, f)),
                  pl.BlockSpec((1, 1, d, FFN_TILE), lambda i, f, ix: (layer, i, 0, f)),
                  pl.BlockSpec((1, 1, FFN_TILE, d), lambda i, f, ix: (layer, i, f, 0))],
        out_specs=pl.BlockSpec((1, cap, d + LANES), lambda i, f, ix: (i, 0, 0)),
        scratch_shapes=[pltpu.VMEM((cap, d), F32), pltpu.VMEM((cap, d), BF16), pltpu.SemaphoreType.DMA(())])
    return pl.pallas_call(
        _ffn_body,
        out_shape=jax.ShapeDtypeStruct((e, cap, d + LANES), F32),
        grid_spec=grid_spec,
        compiler_params=_cparams(("arbitrary", "arbitrary")),
        name="expert_ffn",
    )(idx, h, gate, tok, w_gate, w_up, w_down)


def _combine_body(lo_ref, ys_hbm, x_ref, mod_ref, o_ref, win_ref, xwin_ref, acc_ref, sem, xsem):
    i = pl.program_id(0)
    n_steps = pl.num_programs(0)
    n_exp, cap, width = ys_hbm.shape
    d = width - LANES
    win = COMBINE_WIN
    slot = i % 2
    t0 = i * ROW_TILE

    def first_row(e, tile):
        return pl.multiple_of(jnp.minimum((lo_ref[e, tile] // SUBLANES) * SUBLANES, cap - win), SUBLANES)

    def window_copy(e, tile, buf):
        return pltpu.make_async_copy(ys_hbm.at[e, pl.ds(first_row(e, tile), win), :],
                                     win_ref.at[buf, e], sem.at[buf, e])

    @pl.when(i == 0)
    def _():
        for e in range(n_exp):
            window_copy(e, 0, 0).start()

    @pl.when(i + 1 < n_steps)
    def _():
        for e in range(n_exp):
            window_copy(e, i + 1, 1 - slot).start()

    row_id = lax.broadcasted_iota(jnp.int32, (win, LANES), 0)

    def valid_tok(tok, row0, lo, hi):
        slot_id = row_id + row0
        return jnp.where(jnp.logical_and(slot_id >= lo, slot_id < hi), tok, -1.0)

    def scattered(tok, rows):
        lane_tok = (lax.broadcasted_iota(jnp.int32, tok.shape, 1) + t0).astype(F32)
        onehot = jnp.concatenate([jnp.where(tok == lane_tok + float(j * LANES), 1.0, 0.0)
                                  for j in range(ROW_TILE // LANES)], axis=1).astype(BF16)
        rows_hi, rows_lo = _split2(rows)
        dot_t = lambda a, b: lax.dot_general(a, b, (((0,), (0,)), ((), ())), preferred_element_type=F32)
        return dot_t(onehot, rows_hi) + dot_t(onehot, rows_lo)

    for e in range(n_exp):
        window_copy(e, i, slot).wait()
    toks = jnp.concatenate([valid_tok(win_ref[slot, e, :, d:], first_row(e, i), lo_ref[e, i], lo_ref[e, i + 1])
                            for e in range(n_exp)], axis=0)
    acc_ref[...] = scattered(toks, win_ref[slot, :, :, :d].reshape(n_exp * win, d))

    def per_expert(e, carry):
        lo = lo_ref[e, i]
        hi = lo_ref[e, i + 1]
        row0 = first_row(e, i)

        def extra(w, c):
            want = row0 + w * win
            start = pl.multiple_of(jnp.minimum(want, cap - win), SUBLANES)
            cp = pltpu.make_async_copy(ys_hbm.at[e, pl.ds(start, win), :], xwin_ref, xsem)
            cp.start()
            cp.wait()
            tok = valid_tok(xwin_ref[:, d:], start, jnp.maximum(lo, want), hi)
            acc_ref[...] += scattered(tok, xwin_ref[:, :d])
            return c

        lax.fori_loop(1, (hi - row0 + win - 1) // win, extra, 0)
        return carry

    lax.fori_loop(0, n_exp, per_expert, 0)
    o_ref[...] = x_ref[...] + mod_ref[0][5:6] * acc_ref[...]


def _combine(lo, ysx, x, mod, n_ctx_tiles):
    nt, d = x.shape
    n_exp = ysx.shape[0]
    stream = _stream_of(n_ctx_tiles)
    grid_spec = pltpu.PrefetchScalarGridSpec(
        num_scalar_prefetch=1,
        grid=(nt // ROW_TILE,),
        in_specs=[pl.BlockSpec(memory_space=pl.ANY),
                  pl.BlockSpec((ROW_TILE, d), lambda i, lo: (i, 0)),
                  pl.BlockSpec((1, 8, d), lambda i, lo: (stream(i), 0, 0))],
        out_specs=pl.BlockSpec((ROW_TILE, d), lambda i, lo: (i, 0)),
        scratch_shapes=[pltpu.VMEM((2, n_exp, COMBINE_WIN, d + LANES), F32),
                        pltpu.VMEM((COMBINE_WIN, d + LANES), F32),
                        pltpu.VMEM((ROW_TILE, d), F32),
                        pltpu.SemaphoreType.DMA((2, n_exp)),
                        pltpu.SemaphoreType.DMA(())])
    return pl.pallas_call(
        _combine_body,
        out_shape=jax.ShapeDtypeStruct((nt, d), F32),
        grid_spec=grid_spec,
        compiler_params=_cparams(("arbitrary",)),
        name="moe_combine",
    )(lo, ysx, x, mod)


def _final_body(x_ref, g_ref, o_ref):
    x = x_ref[...]
    o_ref[...] = x * lax.rsqrt(jnp.mean(x * x, axis=-1, keepdims=True) + EPS) * g_ref[...]


def _final_norm(x, g, n_ctx_tiles):
    nt, d = x.shape
    n_tiles = nt // ROW_TILE - n_ctx_tiles
    return pl.pallas_call(
        _final_body,
        out_shape=jax.ShapeDtypeStruct((n_tiles * ROW_TILE, d), F32),
        grid=(n_tiles,),
        in_specs=[pl.BlockSpec((ROW_TILE, d), lambda i: (i + n_ctx_tiles, 0)),
                  pl.BlockSpec((1, d), lambda i: (0, 0))],
        out_specs=pl.BlockSpec((ROW_TILE, d), lambda i: (i, 0)),
        compiler_params=_cparams(("parallel",)),
        name="final_norm",
    )(x, g)


def _route(logits, cap):
    aff = jax.nn.softmax(logits, axis=-1)
    gate, idx = lax.top_k(aff.T, cap)
    idx, gate = lax.sort((idx, gate), dimension=1, num_keys=1)
    return gate, idx


def _moe(x_mid, h2, logits, mod, n_ctx_rows, w_gate, w_up, w_down, layer):
    nt, d = h2.shape
    n_lat = nt - n_ctx_rows
    lg = logits[:, :N_EXPERTS]
    gate_c, idx_c = _route(lg[:n_ctx_rows], EC_CAPACITY * n_ctx_rows // N_EXPERTS)
    gate_l, idx_l = _route(lg[n_ctx_rows:], EC_CAPACITY * n_lat // N_EXPERTS)
    gate = jnp.concatenate([gate_c, gate_l], axis=1)
    idx = jnp.concatenate([idx_c, idx_l + n_ctx_rows], axis=1)
    ysx = _expert_ffn(idx, h2, gate[..., None], idx.astype(F32)[..., None], w_gate, w_up, w_down, layer)
    bounds = jnp.arange(nt // ROW_TILE + 1, dtype=jnp.int32) * ROW_TILE
    lo = jnp.sum(idx[:, :, None] < bounds[None, None, :], axis=1, dtype=jnp.int32)
    return _combine(lo, ysx, x_mid, mod, n_ctx_rows // ROW_TILE)


def _rope_tables(n_ctx_rows, n_lat):
    quarter = DK_C // 4
    inv = ROPE_BASE ** (-jnp.arange(quarter, dtype=F32) / quarter)

    def entries(pos):
        ang = pos.astype(F32)[:, None] * inv[None, :]
        return jnp.concatenate([jnp.cos(ang), jnp.cos(ang), -jnp.sin(ang), jnp.sin(ang)], axis=1)

    ident = jnp.concatenate([jnp.ones((1, LANES), F32), jnp.zeros((1, LANES), F32)], axis=1)
    per = RET_T // GRID_W
    rows = entries(jnp.arange(n_lat // GRID_W)).reshape(n_lat // RET_T, per, 2 * LANES)
    rows = jnp.concatenate([jnp.broadcast_to(ident, (n_ctx_rows // RET_T, per, 2 * LANES)), rows], axis=0)
    row_tab = jnp.pad(rows, ((0, 0), (0, 8 - per), (0, 0)))
    col_tab = jnp.stack([jnp.broadcast_to(ident, (GRID_W, 2 * LANES)), entries(jnp.arange(GRID_W))])
    return row_tab, col_tab


def _pad_cols(a, n):
    return jnp.pad(a, ((0, 0), (0, n - a.shape[1])))


def kernel(x, c, ctx, c_ctx, ada_w, ada_b, norm_g, final_g, ev_w_in, ev_conv, ev_a_log, ev_dt_bias, ev_gdn_norm, ev_hgrn_lb, ev_hgrn_norm, ev_w_out, od_w_in, od_w_out, moe_router, moe_w_gate, moe_w_up, moe_w_down):
    depth = ada_w.shape[0]
    n_lat, d = x.shape[1], x.shape[2]
    n_ctx_rows = ctx.shape[1]
    assert x.shape[0] == 1 and n_ctx_rows % RET_T == 0 and n_lat % RET_T == 0 and RET_T == ROW_TILE
    n_ctx_tiles = n_ctx_rows // ROW_TILE

    xs = jnp.concatenate([ctx[0], x[0]], axis=0)
    cond = jnp.zeros((8, d), F32).at[0].set(jax.nn.silu(c_ctx)).at[1].set(jax.nn.silu(c[0]))
    p_l = jax.nn.softmax(ev_hgrn_lb.astype(F32), axis=1)
    lbs = jnp.cumsum(p_l, axis=1) - p_l[:, :1]
    row_tab, col_tab = _rope_tables(n_ctx_rows, n_lat)
    router = jnp.pad(moe_router, ((0, 0), (0, 0), (0, LANES - N_EXPERTS)))

    wka, wva, wkb, wvb = H_A * DK_A, H_A * DV_A, H_B * DK_B, H_B * DV_B
    n_small = 4 * H_A
    big = 2 * wka + 2 * wva + 3 * wkb + 2 * wvb
    small_at = 2 * wka + 2 * wva

    for l in range(depth):
        last = l == depth - 1
        i = l // 2
        mod = _dense(cond, ada_w[l].astype(BF16), ada_b[l][None, :], 6 * d // 4)
        mod = jnp.pad(mod[:2].reshape(2, 6, d), ((0, 0), (0, 2), (0, 0)))
        g1 = norm_g[l, 0][None, :]
        g2 = norm_g[l, 1][None, :]
        if l % 2 == 0:
            w_in = ev_w_in[i]
            qb_at = small_at + n_small
            fb_at = qb_at + wkb
            ib_at = fb_at + 2 * wkb
            w_perm = jnp.concatenate([w_in[:, :small_at], w_in[:, fb_at:ib_at], w_in[:, qb_at:fb_at],
                                      w_in[:, ib_at:],
                                      _pad_cols(w_in[:, small_at:small_at + n_small], LANES)], axis=1)
            p = _proj(xs, g1, mod, w_perm.astype(BF16), n_ctx_tiles)
            feat = _even_feat(p, jnp.pad(ev_conv[i], ((0, 8 - CONV_K), (0, 0))), n_ctx_tiles)
            par = jnp.zeros((8, LANES), F32)
            par = par.at[0, :2 * H_A].set(-jnp.exp(ev_a_log[i].astype(F32)).reshape(-1))
            par = par.at[1, :2 * H_A].set(ev_dt_bias[i].astype(F32).reshape(-1))
            n_ctx_blocks = n_ctx_rows // (SCAN_T * SCAN_GROUP)
            oa_f, oa_b = _gdn_scan(feat, p, par, big // LANES, n_ctx_blocks)
            lb = jnp.pad(lbs[:, i], ((0, 6), (0, 0)))
            ob_f, ob_b = _gla_scan(p, lb, 6, 2, 7, n_ctx_blocks)
            gains = jnp.zeros((8, LANES), F32).at[0].set(ev_gdn_norm[i]).at[1].set(ev_hgrn_norm[i])
            groups = [(oa_f, oa_b, p, 3), (ob_f, ob_b, p, 8)]
            xs, h2, logits = _outproj(groups, gains, ev_w_out[i].astype(BF16), xs, mod, g2,
                                      router[l], DV_A, n_ctx_tiles)
        else:
            p = _proj(xs, g1, mod, od_w_in[i].astype(BF16), n_ctx_tiles)
            o_f, o_b = _ret_scan(p, row_tab, col_tab, n_ctx_rows // RET_T)
            gains = jnp.ones((8, DV_C), F32)
            groups = [(o_f, o_b, p, 2)]
            xs, h2, logits = _outproj(groups, gains, od_w_out[i].astype(BF16), xs, mod, g2,
                                      router[l], DV_C, n_ctx_tiles)
        xs = _moe(xs, h2, logits, mod, n_ctx_rows, moe_w_gate, moe_w_up, moe_w_down, l)
    return _final_norm(xs, final_g[None, :], n_ctx_tiles)[None]
```
